```python
import math
import jax, jax.numpy as jnp
from jax import lax
import numpy as np

D_MODEL = 2048
BATCH = 2
SEQ = 16384
DEPTH = 2

HEAD_DIM = 64
ATTN_WIDTH = D_MODEL // 2
RWKV_WIDTH = D_MODEL // 4
SSM_WIDTH = D_MODEL - ATTN_WIDTH - RWKV_WIDTH

ATTN_HEADS = ATTN_WIDTH // HEAD_DIM
ATTN_KV_HEADS = ATTN_HEADS // 4
ATTN_GROUP = ATTN_HEADS // ATTN_KV_HEADS
KV_WIDTH = ATTN_KV_HEADS * HEAD_DIM
WINDOW = 128
BLOCK = 128
REL_BUCKETS = 32
REL_MAX_DISTANCE = 128

RWKV_HEADS = RWKV_WIDTH // HEAD_DIM
RWKV_DECAY_LORA = 64
RWKV_A_LORA = 64
RWKV_GATE_LORA = 128
RWKV_COLS = 3 * RWKV_WIDTH + RWKV_DECAY_LORA + RWKV_A_LORA + RWKV_GATE_LORA
RWKV_SPLITS = [RWKV_WIDTH, RWKV_WIDTH + RWKV_DECAY_LORA,
               2 * RWKV_WIDTH + RWKV_DECAY_LORA, 3 * RWKV_WIDTH + RWKV_DECAY_LORA,
               3 * RWKV_WIDTH + RWKV_DECAY_LORA + RWKV_A_LORA]
RWKV_DECAY_SCALE = math.exp(-0.5)
RWKV_LN_EPS = HEAD_DIM * 1e-5

SSM_GROUP_CH = 16
SSM_GROUPS = SSM_WIDTH // SSM_GROUP_CH
SSM_STATE = 64
DT_MIN = 1e-3
DT_MAX = 1e-1

Q_END = ATTN_WIDTH
K_END = Q_END + KV_WIDTH
V_END = K_END + KV_WIDTH
RWKV_END = V_END + RWKV_COLS
IN_COLS = RWKV_END + SSM_WIDTH

D_FF = ((8 * D_MODEL // 3 + 255) // 256) * 256
N_EXPERTS = 8
TOP_K = 2
D_FF_EXPERT = 7 * D_MODEL // 2
N_DENSE = (DEPTH + 1) // 2
N_MOE = DEPTH // 2
NORM_EPS = 1e-6

kernel_name = 'hybrid_swa_rwkv7_s5_moe_block'


def rmsnorm(x, g, eps=NORM_EPS):
    xf = x.astype(jnp.float32)
    y = xf * lax.rsqrt(jnp.mean(xf * xf, -1, keepdims=True) + eps)
    return (y * g.astype(jnp.float32)).astype(x.dtype)


def t5_causal_bucket(dist):
    max_exact = REL_BUCKETS // 2
    d = jnp.maximum(dist, 0)
    large = max_exact + (jnp.log(jnp.maximum(d, 1).astype(jnp.float32) / max_exact)
                         / math.log(REL_MAX_DISTANCE / max_exact)
                         * (REL_BUCKETS - max_exact)).astype(jnp.int32)
    large = jnp.minimum(large, REL_BUCKETS - 1)
    return jnp.where(d < max_exact, d, large)


def sliding_window_attention(q, k, v, q_g, k_g, sinks, rel_bias):
    B, L = q.shape[0], q.shape[1]
    nb = L // BLOCK
    q = rmsnorm(q, q_g).astype(jnp.float32)
    k = rmsnorm(k, k_g).astype(jnp.float32)
    v = v.astype(jnp.float32)
    qb = q.reshape(B, nb, BLOCK, ATTN_KV_HEADS, ATTN_GROUP, HEAD_DIM)

    def band(t):
        tb = t.reshape(B, nb, BLOCK, ATTN_KV_HEADS, HEAD_DIM)
        prev = jnp.concatenate([jnp.zeros_like(tb[:, :1]), tb[:, :-1]], axis=1)
        return jnp.concatenate([prev, tb], axis=2)

    kb, vb = band(k), band(v)
    s = jnp.einsum('bnqhgd,bnkhd->bnhgqk', qb, kb) * (HEAD_DIM ** -0.5)
    qi = jnp.arange(BLOCK, dtype=jnp.int32)[:, None]
    kj = jnp.arange(2 * BLOCK, dtype=jnp.int32)[None, :]
    dist = BLOCK + qi - kj
    bias = rel_bias.astype(jnp.float32)[t5_causal_bucket(dist)]
    bias = jnp.transpose(bias, (2, 0, 1)).reshape(ATTN_KV_HEADS, ATTN_GROUP, BLOCK, 2 * BLOCK)
    s = s + bias
    blk = jnp.arange(nb, dtype=jnp.int32)[:, None, None]
    valid = (dist >= 0) & (dist < WINDOW) & (blk * BLOCK - BLOCK + kj >= 0)
    s = jnp.where(valid[None, :, None, None], s, -jnp.inf)
    sink = sinks.astype(jnp.float32).reshape(1, 1, ATTN_KV_HEADS, ATTN_GROUP, 1, 1)
    m = jnp.maximum(jnp.max(s, -1, keepdims=True), sink)
    p = jnp.exp(s - m)
    denom = jnp.sum(p, -1, keepdims=True) + jnp.exp(sink - m)
    o = jnp.einsum('bnhgqk,bnkhd->bnqhgd', p / denom, vb)
    return o.reshape(B, L, ATTN_WIDTH)


def token_shift(z, mu):
    prev = jnp.pad(z, ((0, 0), (1, 0), (0, 0)))[:, :-1]
    return z + mu * (prev - z)


def rwkv7_time_mix(z, mu, w0, w2, a0, a2, g2, k_k, k_a, r_k, ln_w, ln_b):
    B, L = z.shape[0], z.shape[1]
    H, N = RWKV_HEADS, HEAD_DIM
    f = lambda p: p.astype(jnp.float32)
    z = token_shift(f(z), f(mu))
    r, xw, k, v, xa, xg = jnp.split(z, RWKV_SPLITS, axis=-1)
    decay = jnp.exp(-RWKV_DECAY_SCALE * jax.nn.sigmoid(f(w0) + jnp.tanh(xw) @ f(w2)))
    iclr = jax.nn.sigmoid(f(a0) + xa @ f(a2))
    gate = jax.nn.sigmoid(xg) @ f(g2)
    heads = lambda t: t.reshape(B, L, H, N)
    kk = heads(k * f(k_k))
    kk = kk * lax.rsqrt(jnp.maximum(jnp.sum(kk * kk, -1, keepdims=True), 1e-12))
    k = k * (1.0 + (iclr - 1.0) * f(k_a))
    r_h, k_h, v_h = heads(r), heads(k), heads(v)
    tm = lambda t: jnp.moveaxis(t, 1, 0)
    xs = (tm(r_h), tm(heads(decay)), tm(k_h), tm(v_h), tm(-kk), tm(kk * heads(iclr)))

    def step(S, inp):
        r_t, w_t, k_t, v_t, a_t, b_t = inp
        sa = jnp.einsum('bhij,bhj->bhi', S, a_t)
        S = S * w_t[:, :, None, :] + sa[..., None] * b_t[:, :, None, :] + v_t[..., None] * k_t[:, :, None, :]
        return S, jnp.einsum('bhij,bhj->bhi', S, r_t)

    S0 = jnp.zeros((B, H, N, N), jnp.float32)
    _, y = lax.scan(step, S0, xs)
    y = jnp.moveaxis(y, 0, 1)
    mean = jnp.mean(y, -1, keepdims=True)
    var = jnp.mean(jnp.square(y - mean), -1, keepdims=True)
    y = ((y - mean) * lax.rsqrt(var + RWKV_LN_EPS)).reshape(B, L, RWKV_WIDTH) * f(ln_w) + f(ln_b)
    bonus = jnp.sum(r_h * k_h * f(r_k).reshape(H, N), -1, keepdims=True) * v_h
    y = y + bonus.reshape(B, L, RWKV_WIDTH)
    return y * gate


def s5_ssm(u, lam_re, lam_im, b_re, b_im, c_re, c_im, d, log_dt, glu_w, glu_b):
    B, L = u.shape[0], u.shape[1]
    f = lambda p: p.astype(jnp.float32)
    uf = f(u).reshape(B, L, SSM_GROUPS, SSM_GROUP_CH)
    lr, li = f(lam_re), f(lam_im)
    dt = jnp.exp(f(log_dt))[:, None]
    mag = jnp.exp(lr * dt)
    ab_re = mag * jnp.cos(li * dt)
    ab_im = mag * jnp.sin(li * dt)
    den = lr * lr + li * li
    fr = ((ab_re - 1.0) * lr + ab_im * li) / den
    fi = (ab_im * lr - (ab_re - 1.0) * li) / den
    bb_re = fr[..., None] * f(b_re) - fi[..., None] * f(b_im)
    bb_im = fr[..., None] * f(b_im) + fi[..., None] * f(b_re)
    bu_re = jnp.einsum('blgm,gpm->lbgp', uf, bb_re)
    bu_im = jnp.einsum('blgm,gpm->lbgp', uf, bb_im)
    a_re = jnp.broadcast_to(ab_re, (L, SSM_GROUPS, SSM_STATE))
    a_im = jnp.broadcast_to(ab_im, (L, SSM_GROUPS, SSM_STATE))

    def combine(e1, e2):
        a1r, a1i, b1r, b1i = e1
        a2r, a2i, b2r, b2i = e2
        ar = a2r * a1r - a2i * a1i
        ai = a2r * a1i + a2i * a1r
        br = a2r[:, None] * b1r - a2i[:, None] * b1i + b2r
        bi = a2r[:, None] * b1i + a2i[:, None] * b1r + b2i
        return (ar, ai, br, bi)

    _, _, x_re, x_im = lax.associative_scan(combine, (a_re, a_im, bu_re, bu_im), axis=0)
    y = (jnp.einsum('lbgp,gmp->blgm', x_re, f(c_re))
         - jnp.einsum('lbgp,gmp->blgm', x_im, f(c_im)))
    y = y + f(d).reshape(SSM_GROUPS, SSM_GROUP_CH) * uf
    y = y.reshape(B, L, SSM_WIDTH)
    zg = jax.nn.gelu(y)
    return zg * jax.nn.sigmoid(zg @ f(glu_w) + f(glu_b))


def hybrid_mixer(hn, w_in, q_g, k_g, sinks, rel_bias,
                 r_mu, r_w0, r_w2, r_a0, r_a2, r_g2, r_kk, r_ka, r_rk, r_lnw, r_lnb,
                 s_lre, s_lim, s_bre, s_bim, s_cre, s_cim, s_d, s_dt, s_gw, s_gb,
                 attn_out_g, ssm_out_g, w_out):
    B, L = hn.shape[0], hn.shape[1]
    z = hn @ w_in
    q = z[..., :Q_END].reshape(B, L, ATTN_HEADS, HEAD_DIM)
    k = z[..., Q_END:K_END].reshape(B, L, ATTN_KV_HEADS, HEAD_DIM)
    v = z[..., K_END:V_END].reshape(B, L, ATTN_KV_HEADS, HEAD_DIM)
    attn = sliding_window_attention(q, k, v, q_g, k_g, sinks, rel_bias)
    rw = rwkv7_time_mix(z[..., V_END:RWKV_END], r_mu, r_w0, r_w2, r_a0, r_a2, r_g2,
                        r_kk, r_ka, r_rk, r_lnw, r_lnb)
    ss = s5_ssm(z[..., RWKV_END:], s_lre, s_lim, s_bre, s_bim, s_cre, s_cim, s_d, s_dt, s_gw, s_gb)
    mixed = jnp.concatenate([rmsnorm(attn, attn_out_g).astype(hn.dtype),
                             rw.astype(hn.dtype),
                             rmsnorm(ss, ssm_out_g).astype(hn.dtype)], axis=-1)
    return mixed @ w_out


def swiglu(h, w1, w3, w2):
    return (jax.nn.silu(h @ w1) * (h @ w3)) @ w2


def moe_swiglu(h, router, w1, w3, w2):
    B, L, Dm = h.shape
    t = h.reshape(B * L, Dm)
    logits = t.astype(jnp.float32) @ router.astype(jnp.float32)
    top_v, top_i = lax.top_k(logits, TOP_K)
    gates = jax.nn.softmax(top_v, axis=-1)
    weights = jnp.sum(gates[..., None] * jax.nn.one_hot(top_i, N_EXPERTS, dtype=jnp.float32), axis=1)
    y = jnp.zeros((B * L, Dm), jnp.float32)
    for e in range(N_EXPERTS):
        y = y + weights[:, e:e + 1] * swiglu(t, w1[e], w3[e], w2[e]).astype(jnp.float32)
    return y.reshape(B, L, Dm).astype(h.dtype)


def setup_inputs(seed: int = 0) -> dict:
    key = jax.random.key(seed)
    ks = iter(jax.random.split(key, 64))
    f32 = jnp.float32
    nrm = lambda shape, scale: scale * jax.random.normal(next(ks), shape, f32)
    uni = lambda shape, lo, hi: jax.random.uniform(next(ks), shape, f32, lo, hi)
    n_idx = jnp.arange(SSM_STATE, dtype=f32)
    inp = {}
    inp['x'] = nrm((BATCH, SEQ, D_MODEL), 1.0)
    inp['norm1_g'] = 1.0 + nrm((DEPTH, D_MODEL), 0.05)
    inp['w_in'] = nrm((DEPTH, D_MODEL, IN_COLS), D_MODEL ** -0.5)
    inp['q_norm_g'] = 1.0 + nrm((DEPTH, HEAD_DIM), 0.05)
    inp['k_norm_g'] = 1.0 + nrm((DEPTH, HEAD_DIM), 0.05)
    inp['attn_sinks'] = nrm((DEPTH, ATTN_HEADS), 0.5)
    inp['rel_bias'] = nrm((REL_BUCKETS, ATTN_HEADS), 0.5)
    inp['rwkv_mu'] = uni((DEPTH, RWKV_COLS), 0.0, 1.0)
    inp['rwkv_w0'] = nrm((DEPTH, RWKV_WIDTH), 1.0)
    inp['rwkv_w2'] = nrm((DEPTH, RWKV_DECAY_LORA, RWKV_WIDTH), 0.1 * RWKV_DECAY_LORA ** -0.5)
    inp['rwkv_a0'] = nrm((DEPTH, RWKV_WIDTH), 0.1)
    inp['rwkv_a2'] = nrm((DEPTH, RWKV_A_LORA, RWKV_WIDTH), 0.1 * RWKV_A_LORA ** -0.5)
    inp['rwkv_g2'] = nrm((DEPTH, RWKV_GATE_LORA, RWKV_WIDTH), RWKV_GATE_LORA ** -0.5)
    inp['rwkv_k_k'] = 0.85 + nrm((DEPTH, RWKV_WIDTH), 0.05)
    inp['rwkv_k_a'] = 1.0 + nrm((DEPTH, RWKV_WIDTH), 0.05)
    inp['rwkv_r_k'] = nrm((DEPTH, RWKV_WIDTH), 0.1)
    inp['rwkv_ln_w'] = 1.0 + nrm((DEPTH, RWKV_WIDTH), 0.05)
    inp['rwkv_ln_b'] = nrm((DEPTH, RWKV_WIDTH), 0.01)
    inp['ssm_lambda_re'] = -0.5 + nrm((DEPTH, SSM_GROUPS, SSM_STATE), 0.01)
    inp['ssm_lambda_im'] = jnp.pi * n_idx + nrm((DEPTH, SSM_GROUPS, SSM_STATE), 0.01)
    inp['ssm_b_re'] = nrm((DEPTH, SSM_GROUPS, SSM_STATE, SSM_GROUP_CH), (2 * SSM_GROUP_CH) ** -0.5)
    inp['ssm_b_im'] = nrm((DEPTH, SSM_GROUPS, SSM_STATE, SSM_GROUP_CH), (2 * SSM_GROUP_CH) ** -0.5)
    inp['ssm_c_re'] = nrm((DEPTH, SSM_GROUPS, SSM_GROUP_CH, SSM_STATE), (2 * SSM_STATE) ** -0.5)
    inp['ssm_c_im'] = nrm((DEPTH, SSM_GROUPS, SSM_GROUP_CH, SSM_STATE), (2 * SSM_STATE) ** -0.5)
    inp['ssm_d'] = nrm((DEPTH, SSM_WIDTH), 1.0)
    inp['ssm_log_dt'] = uni((DEPTH, SSM_GROUPS), math.log(DT_MIN), math.log(DT_MAX))
    inp['ssm_glu_w'] = nrm((DEPTH, SSM_WIDTH, SSM_WIDTH), SSM_WIDTH ** -0.5)
    inp['ssm_glu_b'] = nrm((DEPTH, SSM_WIDTH), 0.01)
    inp['attn_out_g'] = 1.0 + nrm((DEPTH, ATTN_WIDTH), 0.05)
    inp['ssm_out_g'] = 1.0 + nrm((DEPTH, SSM_WIDTH), 0.05)
    inp['w_out'] = nrm((DEPTH, D_MODEL, D_MODEL), D_MODEL ** -0.5)
    inp['norm2_g'] = 1.0 + nrm((DEPTH, D_MODEL), 0.05)
    inp['ffn_w1'] = nrm((N_DENSE, D_MODEL, D_FF), D_MODEL ** -0.5)
    inp['ffn_w3'] = nrm((N_DENSE, D_MODEL, D_FF), D_MODEL ** -0.5)
    inp['ffn_w2'] = nrm((N_DENSE, D_FF, D_MODEL), D_FF ** -0.5)
    inp['moe_router'] = nrm((N_MOE, D_MODEL, N_EXPERTS), D_MODEL ** -0.5)
    inp['moe_w1'] = nrm((N_MOE, N_EXPERTS, D_MODEL, D_FF_EXPERT), D_MODEL ** -0.5)
    inp['moe_w3'] = nrm((N_MOE, N_EXPERTS, D_MODEL, D_FF_EXPERT), D_MODEL ** -0.5)
    inp['moe_w2'] = nrm((N_MOE, N_EXPERTS, D_FF_EXPERT, D_MODEL), D_FF_EXPERT ** -0.5)
    return inp


def reference(x, norm1_g, w_in, q_norm_g, k_norm_g, attn_sinks, rel_bias,
              rwkv_mu, rwkv_w0, rwkv_w2, rwkv_a0, rwkv_a2, rwkv_g2, rwkv_k_k, rwkv_k_a,
              rwkv_r_k, rwkv_ln_w, rwkv_ln_b,
              ssm_lambda_re, ssm_lambda_im, ssm_b_re, ssm_b_im, ssm_c_re, ssm_c_im,
              ssm_d, ssm_log_dt, ssm_glu_w, ssm_glu_b,
              attn_out_g, ssm_out_g, w_out, norm2_g,
              ffn_w1, ffn_w3, ffn_w2, moe_router, moe_w1, moe_w3, moe_w2):
    h = x
    for i in range(DEPTH):
        hn = rmsnorm(h, norm1_g[i])
        mix = hybrid_mixer(hn, w_in[i], q_norm_g[i], k_norm_g[i], attn_sinks[i], rel_bias,
                           rwkv_mu[i], rwkv_w0[i], rwkv_w2[i], rwkv_a0[i], rwkv_a2[i], rwkv_g2[i],
                           rwkv_k_k[i], rwkv_k_a[i], rwkv_r_k[i], rwkv_ln_w[i], rwkv_ln_b[i],
                           ssm_lambda_re[i], ssm_lambda_im[i], ssm_b_re[i], ssm_b_im[i],
                           ssm_c_re[i], ssm_c_im[i], ssm_d[i], ssm_log_dt[i],
                           ssm_glu_w[i], ssm_glu_b[i], attn_out_g[i], ssm_out_g[i], w_out[i])
        h = h + mix.astype(h.dtype)
        hn = rmsnorm(h, norm2_g[i])
        j = i // 2
        if i % 2 == 0:
            ff = swiglu(hn, ffn_w1[j], ffn_w3[j], ffn_w2[j])
        else:
            ff = moe_swiglu(hn, moe_router[j], moe_w1[j], moe_w3[j], moe_w2[j])
        h = h + ff.astype(h.dtype)
    return h
```

```python
import functools
import math

import numpy as np
import jax
import jax.numpy as jnp
from jax import lax
from jax.experimental import pallas as pl
from jax.experimental.pallas import tpu as pltpu

F32 = jnp.float32
BF16 = jnp.bfloat16

D_MODEL = 2048
HEAD_DIM = 64
ATTN_WIDTH = 1024
RWKV_WIDTH = 512
SSM_WIDTH = 512
ATTN_HEADS = 16
ATTN_KV_HEADS = 4
ATTN_GROUP = 4
KV_WIDTH = 256
WINDOW = 128
BLOCK = 128
REL_BUCKETS = 32
REL_MAX_DISTANCE = 128
RWKV_HEADS = 8
RWKV_DECAY_LORA = 64
RWKV_A_LORA = 64
RWKV_GATE_LORA = 128
RWKV_COLS = 1792
RWKV_DECAY_SCALE = math.exp(-0.5)
RWKV_LN_EPS = HEAD_DIM * 1e-5
SSM_GROUP_CH = 16
SSM_GROUPS = 32
SSM_STATE = 64
SSM_STATES = SSM_GROUPS * SSM_STATE
Q_END = ATTN_WIDTH
K_END = Q_END + KV_WIDTH
V_END = K_END + KV_WIDTH
RWKV_END = V_END + RWKV_COLS
IN_COLS = RWKV_END + SSM_WIDTH
D_FF = 5632
N_EXPERTS = 8
TOP_K = 2
D_FF_EXPERT = 7168
NORM_EPS = 1e-6

LANES = 128
MXU_DIM = 256
VMEM_LIMIT = 56 * 1024 * 1024

ROW_TILE = 512
FFN_COL_TILE = 512
SCAN_TILE = 256
SCAN_PAD = 128
RWKV_CHUNK = 64
RWKV_QUAD = 4
GATHER_TILE = 256
ROUTER_LANES = 128


def _cparams(sem):
    return pltpu.CompilerParams(dimension_semantics=sem, vmem_limit_bytes=VMEM_LIMIT)


def _mm(a, b):
    return jnp.dot(a.astype(BF16), b.astype(BF16), preferred_element_type=F32)


def _mm_nt(a, b):
    return lax.dot_general(a.astype(BF16), b.astype(BF16), (((1,), (1,)), ((), ())),
                           preferred_element_type=F32)


def _mm_tn(a, b):
    return lax.dot_general(a.astype(BF16), b.astype(BF16), (((0,), (0,)), ((), ())),
                           preferred_element_type=F32)


def _split2(x):
    hi = x.astype(BF16)
    lo = (x - hi.astype(F32)).astype(BF16)
    return hi, lo


def _split3(x):
    h1 = x.astype(BF16)
    r1 = x - h1.astype(F32)
    h2 = r1.astype(BF16)
    h3 = (r1 - h2.astype(F32)).astype(BF16)
    return h1, h2, h3


def _seg_sum(x, ones_bd):
    hi, lo = _split2(x)
    return (jnp.dot(hi, ones_bd, preferred_element_type=F32)
            + jnp.dot(lo, ones_bd, preferred_element_type=F32))


def _const_spec(shape):
    nd = len(shape)
    return pl.BlockSpec(shape, lambda *_: (0,) * nd, pipeline_mode=pl.Buffered(1))


def _in_proj_kernel(x_ref, g_ref, w_ref, q_ref, kv_ref, rw_ref, ss_ref):
    x = x_ref[...]
    ms = jnp.mean(x * x, axis=-1, keepdims=True)
    hn = (x * lax.rsqrt(ms + NORM_EPS) * g_ref[...]).astype(BF16)
    q_ref[...] = jnp.dot(hn, w_ref[:, 0:Q_END], preferred_element_type=F32)
    kv_ref[...] = jnp.dot(hn, w_ref[:, Q_END:V_END], preferred_element_type=F32)
    rw_ref[...] = jnp.dot(hn, w_ref[:, V_END:RWKV_END], preferred_element_type=F32)
    ss_ref[...] = jnp.dot(hn, w_ref[:, RWKV_END:IN_COLS], preferred_element_type=F32)


def _in_proj(h, g, w):
    t = h.shape[0]
    tm = min(ROW_TILE, t)
    row = lambda w_: pl.BlockSpec((tm, w_), lambda i: (i, 0))
    return pl.pallas_call(
        _in_proj_kernel,
        grid=(t // tm,),
        in_specs=[row(D_MODEL), _const_spec((1, D_MODEL)), _const_spec((D_MODEL, IN_COLS))],
        out_specs=[row(ATTN_WIDTH), row(2 * KV_WIDTH), row(RWKV_COLS), row(SSM_WIDTH)],
        out_shape=[jax.ShapeDtypeStruct((t, ATTN_WIDTH), F32),
                   jax.ShapeDtypeStruct((t, 2 * KV_WIDTH), F32),
                   jax.ShapeDtypeStruct((t, RWKV_COLS), F32),
                   jax.ShapeDtypeStruct((t, SSM_WIDTH), F32)],
        compiler_params=_cparams(("arbitrary",)),
        name="in_proj",
    )(h, g.reshape(1, D_MODEL), w)


def _attn_kernel(sink_ref, q_ref, kvp_ref, kvc_ref, bias_ref, qg_ref, kg_ref, og_ref, e_ref,
                 o_ref, acc_ref):
    first = pl.program_id(1) == 0
    e = e_ref[...]
    kv = jnp.concatenate([kvp_ref[...], kvc_ref[...]], axis=0)

    def head_norm(t, g):
        ss = _seg_sum(t * t, e)
        return t * lax.rsqrt(ss * (1.0 / HEAD_DIM) + NORM_EPS) * g

    kn = head_norm(kv[:, :KV_WIDTH], kg_ref[...]).astype(BF16)
    vb = kv[:, KV_WIDTH:].astype(BF16)
    key_idx = lax.broadcasted_iota(jnp.int32, (BLOCK, 2 * BLOCK), 1)
    no_prev = jnp.logical_and(first, key_idx < BLOCK)
    scale = HEAD_DIM ** -0.5
    for h in range(ATTN_KV_HEADS):
        qn = (head_norm(q_ref[:, h * MXU_DIM:(h + 1) * MXU_DIM], qg_ref[...]) * scale).astype(BF16)
        kh = kn[:, h * HEAD_DIM:(h + 1) * HEAD_DIM]
        vh = vb[:, h * HEAD_DIM:(h + 1) * HEAD_DIM]
        for g in range(ATTN_GROUP):
            head = h * ATTN_GROUP + g
            s = _mm_nt(qn[:, g * HEAD_DIM:(g + 1) * HEAD_DIM], kh) + bias_ref[head]
            s = jnp.where(no_prev, -jnp.inf, s)
            sink = sink_ref[head]
            m = jnp.maximum(jnp.max(s, axis=-1, keepdims=True), sink)
            p = jnp.exp(s - m)
            denom = jnp.sum(p, axis=-1, keepdims=True) + jnp.exp(sink - m)
            o = jnp.dot(p.astype(BF16), vh, preferred_element_type=F32) / denom
            acc_ref[:, head * HEAD_DIM:(head + 1) * HEAD_DIM] = o
    o = acc_ref[...]
    ms = jnp.mean(o * o, axis=-1, keepdims=True)
    o_ref[...] = (o * lax.rsqrt(ms + NORM_EPS) * og_ref[...]).astype(BF16)


def _t5_bucket_table():
    qi = np.arange(BLOCK)[:, None]
    kj = np.arange(2 * BLOCK)[None, :]
    dist = BLOCK + qi - kj
    max_exact = REL_BUCKETS // 2
    d = np.maximum(dist, 0)
    large = max_exact + (np.log(np.maximum(d, 1).astype(np.float32) / max_exact)
                         / math.log(REL_MAX_DISTANCE / max_exact)
                         * (REL_BUCKETS - max_exact)).astype(np.int32)
    large = np.minimum(large, REL_BUCKETS - 1)
    bucket = np.where(d < max_exact, d, large)
    valid = (dist >= 0) & (dist < WINDOW)
    return bucket, valid


def _attention(q, kv, batch, seq, q_g, k_g, sinks, rel_bias, out_g):
    nb = seq // BLOCK
    bucket, valid = _t5_bucket_table()
    bias = jnp.transpose(rel_bias.astype(F32)[bucket], (2, 0, 1))
    bias = jnp.where(valid[None], bias, -jnp.inf)
    seg = np.arange(MXU_DIM) // HEAD_DIM
    ones_bd = jnp.asarray(seg[:, None] == seg[None, :], dtype=BF16)
    tile4 = lambda g: jnp.tile(g.astype(F32), ATTN_GROUP).reshape(1, MXU_DIM)
    grid_spec = pltpu.PrefetchScalarGridSpec(
        num_scalar_prefetch=1,
        grid=(batch, nb),
        in_specs=[
            pl.BlockSpec((BLOCK, ATTN_WIDTH), lambda b, n, s: (b * nb + n, 0)),
            pl.BlockSpec((BLOCK, 2 * KV_WIDTH), lambda b, n, s: (jnp.maximum(b * nb + n - 1, 0), 0)),
            pl.BlockSpec((BLOCK, 2 * KV_WIDTH), lambda b, n, s: (b * nb + n, 0)),
            pl.BlockSpec((ATTN_HEADS, BLOCK, 2 * BLOCK), lambda b, n, s: (0, 0, 0)),
            pl.BlockSpec((1, MXU_DIM), lambda b, n, s: (0, 0)),
            pl.BlockSpec((1, MXU_DIM), lambda b, n, s: (0, 0)),
            pl.BlockSpec((1, ATTN_WIDTH), lambda b, n, s: (0, 0)),
            pl.BlockSpec((MXU_DIM, MXU_DIM), lambda b, n, s: (0, 0)),
        ],
        out_specs=pl.BlockSpec((BLOCK, ATTN_WIDTH), lambda b, n, s: (b * nb + n, 0)),
        scratch_shapes=[pltpu.VMEM((BLOCK, ATTN_WIDTH), F32)],
    )
    return pl.pallas_call(
        _attn_kernel,
        grid_spec=grid_spec,
        out_shape=jax.ShapeDtypeStruct((batch * seq, ATTN_WIDTH), BF16),
        compiler_params=_cparams(("arbitrary", "arbitrary")),
        name="swa",
    )(sinks.astype(F32), q, kv, kv, bias, tile4(q_g), tile4(k_g), out_g.reshape(1, ATTN_WIDTH), ones_bd)


def _rwkv_chunk(r, lw, cum, k, v, a, b, ht, consts):
    masks, tri_strict, lvl_masks, diag2, eye, cat_strict, cat_incl, bd_mask = consts
    c = r.shape[0]
    last = cum[c - 1:c]
    a_t = a * jnp.exp(cum - lw)
    r_t = r * jnp.exp(cum)
    p_inv = jnp.exp(-cum)
    b_t = b * p_inv
    k_t = k * p_inv
    ratio = jnp.exp(last - cum)
    stack = lambda x: jnp.concatenate([x * m for m in masks], axis=0)
    g_heads = len(masks)
    n_st = _mm_nt(stack(a_t), b_t).reshape(g_heads, c, c)
    n_mat = jnp.where(tri_strict[None], n_st, 0.0)
    t_inv = eye[None] + jnp.where(diag2[None], n_mat, 0.0)
    bdims = (((2,), (1,)), ((0,), (0,)))
    for lm in lvl_masks:
        n_k = jnp.where(lm[None], n_mat, 0.0).astype(BF16)
        tb = t_inv.astype(BF16)
        tn = lax.dot_general(tb, n_k, bdims, preferred_element_type=F32)
        t_inv = t_inv + lax.dot_general(tn.astype(BF16), tb, bdims, preferred_element_type=F32)
    k_stack = stack(k_t)
    v_stack = stack(v)
    a_k = jnp.where(cat_strict, _mm_nt(a_t, k_stack), 0.0)
    r_b = jnp.where(cat_incl, _mm_nt(r_t, stack(b_t)), 0.0)
    r_k = jnp.where(cat_incl, _mm_nt(r_t, k_stack), 0.0)
    w0 = _mm_nt(a_t, ht) + _mm(a_k, v_stack)
    u = masks[0] * _mm(t_inv[0], w0)
    for g in range(1, g_heads):
        u = u + masks[g] * _mm(t_inv[g], w0)
    y = _mm_nt(r_t, ht) + _mm(r_b, stack(u)) + _mm(r_k, v_stack)
    upd = _mm_tn(jnp.concatenate([u, v], axis=0), jnp.concatenate([b * ratio, k * ratio], axis=0))
    h_new = ht * jnp.exp(last) + jnp.where(bd_mask, upd, 0.0)
    return y, h_new


def _rwkv_consts(c, g_heads):
    w = g_heads * HEAD_DIM
    lane = lax.broadcasted_iota(jnp.int32, (1, w), 1) // HEAD_DIM
    masks = [(lane == g).astype(F32) for g in range(g_heads)]
    row = lax.broadcasted_iota(jnp.int32, (c, c), 0)
    col = lax.broadcasted_iota(jnp.int32, (c, c), 1)
    tri_strict = col < row
    diag2 = jnp.logical_and((row // 2) == (col // 2), tri_strict)
    eye = (row == col).astype(F32)
    lvl_masks = []
    bs = 2
    while bs < c:
        lvl_masks.append(((row // (2 * bs)) == (col // (2 * bs))) & ((row // bs) % 2 == 1)
                         & ((col // bs) % 2 == 0))
        bs *= 2
    crow = lax.broadcasted_iota(jnp.int32, (c, g_heads * c), 0)
    ccol = lax.broadcasted_iota(jnp.int32, (c, g_heads * c), 1) % c
    hrow = lax.broadcasted_iota(jnp.int32, (w, w), 0) // HEAD_DIM
    hcol = lax.broadcasted_iota(jnp.int32, (w, w), 1) // HEAD_DIM
    return masks, tri_strict, lvl_masks, diag2, eye, ccol < crow, ccol <= crow, hrow == hcol


def _rwkv_kernel(z_ref, mu_ref, w2a_ref, g2_ref, w0_ref, a0_ref, kk_ref, ka_ref, rk_ref, lnw_ref,
                 lnb_ref, e_ref, tri_ref, o_ref,
                 carry_ref, r_s, lw_s, cum_s, k_s, v_s, a_s, b_s, y_s, h_s):
    tb = z_ref.shape[0]
    wd = RWKV_WIDTH
    quad_w = RWKV_QUAD * HEAD_DIM
    n_quads = RWKV_HEADS // RWKV_QUAD

    @pl.when(pl.program_id(1) == 0)
    def _():
        carry_ref[...] = jnp.zeros_like(carry_ref)
        h_s[...] = jnp.zeros_like(h_s)

    z = z_ref[...]
    rows = lax.broadcasted_iota(jnp.int32, (tb, 1), 0)
    prev = jnp.where(rows == 0, carry_ref[...], pltpu.roll(z, 1, axis=0))
    carry_ref[...] = z[tb - 1:tb]
    zs = z + mu_ref[...] * (prev - z)
    r = zs[:, 0:wd]
    k = zs[:, wd:2 * wd]
    v = zs[:, 2 * wd:3 * wd]
    xg = zs[:, 3 * wd:3 * wd + RWKV_GATE_LORA]
    xwa = zs[:, 3 * wd + RWKV_GATE_LORA:]
    lane = lax.broadcasted_iota(jnp.int32, (1, RWKV_DECAY_LORA + RWKV_A_LORA), 1)
    lora_in = jnp.where(lane < RWKV_DECAY_LORA, jnp.tanh(xwa), xwa)
    lora = _mm(lora_in, w2a_ref[...])
    lw = -RWKV_DECAY_SCALE * jax.nn.sigmoid(w0_ref[...] + lora[:, :wd])
    iclr = jax.nn.sigmoid(a0_ref[...] + lora[:, wd:])
    gate = _mm(jax.nn.sigmoid(xg), g2_ref[...])
    kk = k * kk_ref[...]
    e = e_ref[...]
    seg = lambda x: jnp.concatenate(
        [_seg_sum(x[:, q * quad_w:(q + 1) * quad_w], e) for q in range(n_quads)], axis=1)
    kk = kk * lax.rsqrt(jnp.maximum(seg(kk * kk), 1e-12))
    k2 = k * (1.0 + (iclr - 1.0) * ka_ref[...])
    l1, l2, l3 = _split3(lw)
    tri = tri_ref[...]
    cum = (jnp.dot(tri, l1, preferred_element_type=F32) + jnp.dot(tri, l2, preferred_element_type=F32)
           + jnp.dot(tri, l3, preferred_element_type=F32))
    r_s[...] = r
    lw_s[...] = lw
    cum_s[...] = cum
    k_s[...] = k2
    v_s[...] = v
    a_s[...] = -kk
    b_s[...] = kk * iclr

    consts = _rwkv_consts(RWKV_CHUNK, RWKV_QUAD)

    def chunk_body(ci, carry):
        rs = pl.ds(pl.multiple_of(ci * RWKV_CHUNK, RWKV_CHUNK), RWKV_CHUNK)
        for q in range(n_quads):
            cs = slice(q * quad_w, (q + 1) * quad_w)
            y, h_new = _rwkv_chunk(r_s[rs, cs], lw_s[rs, cs], cum_s[rs, cs], k_s[rs, cs], v_s[rs, cs],
                                   a_s[rs, cs], b_s[rs, cs], h_s[q], consts)
            y_s[rs, cs] = y
            h_s[q] = h_new
        return carry

    lax.fori_loop(0, tb // RWKV_CHUNK, chunk_body, 0)

    y = y_s[...]
    mean = seg(y) * (1.0 / HEAD_DIM)
    d = y - mean
    var = seg(d * d) * (1.0 / HEAD_DIM)
    yn = d * lax.rsqrt(var + RWKV_LN_EPS) * lnw_ref[...] + lnb_ref[...]
    bonus = seg(r * k2 * rk_ref[...]) * v
    o_ref[...] = ((yn + bonus) * gate).astype(BF16)


def _rwkv(z, batch, seq, mu, w0, w2, a0, a2, g2, k_k, k_a, r_k, ln_w, ln_b):
    tb = min(SCAN_TILE, seq)
    nb = seq // tb
    wd = RWKV_WIDTH
    quad_w = RWKV_QUAD * HEAD_DIM
    w2a = jnp.zeros((RWKV_DECAY_LORA + RWKV_A_LORA, 2 * wd), F32)
    w2a = w2a.at[:RWKV_DECAY_LORA, :wd].set(w2).at[RWKV_DECAY_LORA:, wd:].set(a2).astype(BF16)
    seg = np.arange(quad_w) // HEAD_DIM
    ones_bd = jnp.asarray(seg[:, None] == seg[None, :], dtype=BF16)
    t_idx = np.arange(tb)
    tri = jnp.asarray((t_idx[:, None] // RWKV_CHUNK == t_idx[None, :] // RWKV_CHUNK)
                      & (t_idx[None, :] <= t_idx[:, None]), dtype=BF16)
    vec = lambda p: p.astype(F32).reshape(1, -1)
    buf = lambda: pltpu.VMEM((tb, wd), F32)
    return pl.pallas_call(
        _rwkv_kernel,
        grid=(batch, nb),
        in_specs=[pl.BlockSpec((tb, RWKV_COLS), lambda b, j: (b * nb + j, 0)),
                  _const_spec((1, RWKV_COLS)),
                  _const_spec((RWKV_DECAY_LORA + RWKV_A_LORA, 2 * wd)),
                  _const_spec((RWKV_GATE_LORA, wd))]
                 + [_const_spec((1, wd))] * 7
                 + [_const_spec((quad_w, quad_w)), _const_spec((tb, tb))],
        out_specs=pl.BlockSpec((tb, wd), lambda b, j: (b * nb + j, 0)),
        out_shape=jax.ShapeDtypeStruct((batch * seq, wd), BF16),
        scratch_shapes=[pltpu.VMEM((1, RWKV_COLS), F32)] + [buf() for _ in range(8)]
                       + [pltpu.VMEM((RWKV_HEADS // RWKV_QUAD, quad_w, quad_w), F32)],
        compiler_params=_cparams(("arbitrary", "arbitrary")),
        name="rwkv7",
    )(z, vec(mu), w2a, g2.astype(BF16), vec(w0), vec(a0), vec(k_k), vec(k_a), vec(r_k), vec(ln_w),
      vec(ln_b), ones_bd, tri)


def _s5_kernel(u_ref, bbr_ref, bbi_ref, cr_ref, ci_ref, apr_ref, api_ref, a2r_ref, a2i_ref, d_ref,
               gw_ref, gb_ref, og_ref, o_ref, xr_s, xi_s, yr_s, yi_s, car_s, cai_s):
    tb = u_ref.shape[0]
    pad = SCAN_PAD

    @pl.when(pl.program_id(1) == 0)
    def _():
        car_s[...] = jnp.zeros_like(car_s)
        cai_s[...] = jnp.zeros_like(cai_s)
        for s_ref in (xr_s, xi_s, yr_s, yi_s):
            s_ref[0:pad, :] = jnp.zeros((pad, SSM_STATES), F32)

    u = u_ref[...]
    ub = u.astype(BF16)
    xr_s[pad:pad + tb, :] = jnp.dot(ub, bbr_ref[...], preferred_element_type=F32)
    xi_s[pad:pad + tb, :] = jnp.dot(ub, bbi_ref[...], preferred_element_type=F32)
    src, dst = (xr_s, xi_s), (yr_s, yi_s)
    n_steps = a2r_ref.shape[0]
    for step in range(n_steps):
        sh = 1 << step
        ar = a2r_ref[step:step + 1, :]
        ai = a2i_ref[step:step + 1, :]
        cr_ = src[0][pad:pad + tb, :]
        ci_ = src[1][pad:pad + tb, :]
        sr = src[0][pad - sh:pad - sh + tb, :]
        si = src[1][pad - sh:pad - sh + tb, :]
        dst[0][pad:pad + tb, :] = cr_ + ar * sr - ai * si
        dst[1][pad:pad + tb, :] = ci_ + ar * si + ai * sr
        src, dst = dst, src
    xr = src[0][pad:pad + tb, :]
    xi = src[1][pad:pad + tb, :]
    car = car_s[...]
    cai = cai_s[...]
    apr = apr_ref[...]
    api = api_ref[...]
    xr, xi = xr + apr * car - api * cai, xi + apr * cai + api * car
    car_s[...] = xr[tb - 1:tb]
    cai_s[...] = xi[tb - 1:tb]
    y = (jnp.dot(xr.astype(BF16), cr_ref[...], preferred_element_type=F32)
         - jnp.dot(xi.astype(BF16), ci_ref[...], preferred_element_type=F32) + d_ref[...] * u)
    zg = jax.nn.gelu(y)
    gl = _mm(zg, gw_ref[...]) + gb_ref[...]
    out = zg * jax.nn.sigmoid(gl)
    ms = jnp.mean(out * out, axis=-1, keepdims=True)
    o_ref[...] = (out * lax.rsqrt(ms + NORM_EPS) * og_ref[...]).astype(BF16)


def _s5(u, batch, seq, lam_re, lam_im, b_re, b_im, c_re, c_im, d, log_dt, glu_w, glu_b, out_g):
    tb = min(SCAN_TILE, seq)
    nb = seq // tb
    n_steps = int(math.log2(tb))
    f = lambda p: p.astype(F32)
    lr, li = f(lam_re), f(lam_im)
    dt = jnp.exp(f(log_dt))[:, None]
    mag = jnp.exp(lr * dt)
    ab_re = mag * jnp.cos(li * dt)
    ab_im = mag * jnp.sin(li * dt)
    den = lr * lr + li * li
    fr = ((ab_re - 1.0) * lr + ab_im * li) / den
    fi = (ab_im * lr - (ab_re - 1.0) * li) / den
    bb_re = fr[..., None] * f(b_re) - fi[..., None] * f(b_im)
    bb_im = fr[..., None] * f(b_im) + fi[..., None] * f(b_re)
    eye = jnp.eye(SSM_GROUPS, dtype=F32)
    in_bd = lambda t: jnp.einsum('gpm,gh->gmhp', t, eye).reshape(SSM_WIDTH, SSM_STATES).astype(BF16)
    out_bd = lambda t: jnp.einsum('gmp,gh->gphm', f(t), eye).reshape(SSM_STATES, SSM_WIDTH).astype(BF16)

    def a_pow(n):
        n = n.astype(F32)[..., None, None]
        m = jnp.exp(n * (lr * dt))
        ang = n * (li * dt)
        shape = n.shape[:-2] + (SSM_STATES,)
        return (m * jnp.cos(ang)).reshape(shape), (m * jnp.sin(ang)).reshape(shape)

    apr, api = a_pow(jnp.arange(1, tb + 1))
    a2r, a2i = a_pow(2 ** jnp.arange(n_steps))
    vec = lambda p: f(p).reshape(1, -1)
    buf = lambda: pltpu.VMEM((SCAN_PAD + tb, SSM_STATES), F32)
    return pl.pallas_call(
        _s5_kernel,
        grid=(batch, nb),
        in_specs=[pl.BlockSpec((tb, SSM_WIDTH), lambda b, j: (b * nb + j, 0)),
                  _const_spec((SSM_WIDTH, SSM_STATES)), _const_spec((SSM_WIDTH, SSM_STATES)),
                  _const_spec((SSM_STATES, SSM_WIDTH)), _const_spec((SSM_STATES, SSM_WIDTH)),
                  _const_spec((tb, SSM_STATES)), _const_spec((tb, SSM_STATES)),
                  _const_spec((n_steps, SSM_STATES)), _const_spec((n_steps, SSM_STATES)),
                  _const_spec((1, SSM_WIDTH)), _const_spec((SSM_WIDTH, SSM_WIDTH)),
                  _const_spec((1, SSM_WIDTH)), _const_spec((1, SSM_WIDTH))],
        out_specs=pl.BlockSpec((tb, SSM_WIDTH), lambda b, j: (b * nb + j, 0)),
        out_shape=jax.ShapeDtypeStruct((batch * seq, SSM_WIDTH), BF16),
        scratch_shapes=[buf(), buf(), buf(), buf(),
                        pltpu.VMEM((1, SSM_STATES), F32), pltpu.VMEM((1, SSM_STATES), F32)],
        compiler_params=_cparams(("arbitrary", "arbitrary")),
        name="s5",
    )(u, in_bd(bb_re), in_bd(bb_im), out_bd(c_re), out_bd(c_im), apr, api, a2r, a2i, vec(d),
      glu_w.astype(BF16), vec(glu_b), vec(out_g))


def _out_proj_kernel(h_ref, at_ref, rw_ref, ss_ref, w_ref, g_ref, *rest, with_router):
    if with_router:
        rt_ref, ho_ref, hn_ref, idx_ref, gate_ref = rest
    else:
        ho_ref, hn_ref = rest
    mix = (jnp.dot(at_ref[...], w_ref[0:ATTN_WIDTH, :], preferred_element_type=F32)
           + jnp.dot(rw_ref[...], w_ref[ATTN_WIDTH:ATTN_WIDTH + RWKV_WIDTH, :], preferred_element_type=F32)
           + jnp.dot(ss_ref[...], w_ref[ATTN_WIDTH + RWKV_WIDTH:, :], preferred_element_type=F32))
    h = h_ref[...] + mix
    ho_ref[...] = h
    ms = jnp.mean(h * h, axis=-1, keepdims=True)
    hn = h * lax.rsqrt(ms + NORM_EPS) * g_ref[...]
    if not with_router:
        hn_ref[...] = hn.astype(BF16)
        return
    hn_ref[...] = hn
    h_hi, h_lo = _split2(hn)
    r_hi, r_lo = _split2(rt_ref[...])
    logits = (jnp.dot(h_hi, r_hi, preferred_element_type=F32)
              + jnp.dot(h_lo, r_hi, preferred_element_type=F32)
              + jnp.dot(h_hi, r_lo, preferred_element_type=F32))
    lane_i = lax.broadcasted_iota(jnp.int32, logits.shape, 1)
    lane = lane_i.astype(F32)
    logits = jnp.where(lane_i < N_EXPERTS, logits, -jnp.inf)
    m1 = jnp.max(logits, axis=-1, keepdims=True)
    i1 = jnp.min(jnp.where(logits == m1, lane, float(ROUTER_LANES)), axis=-1, keepdims=True)
    rest_l = jnp.where(lane == i1, -jnp.inf, logits)
    m2 = jnp.max(rest_l, axis=-1, keepdims=True)
    i2 = jnp.min(jnp.where(rest_l == m2, lane, float(ROUTER_LANES)), axis=-1, keepdims=True)
    e2 = jnp.exp(m2 - m1)
    g1 = 1.0 / (1.0 + e2)
    g2 = e2 / (1.0 + e2)
    idx_ref[...] = jnp.where(lane_i == 0, i1, jnp.where(lane_i == 1, i2, 0.0)).astype(jnp.int32)
    gate_ref[...] = jnp.where(lane_i == 0, g1, jnp.where(lane_i == 1, g2, 0.0))


def _out_proj(h, attn, rw, ss, w_out, g2, router=None):
    t = h.shape[0]
    tm = min(ROW_TILE, t)
    row = lambda w_: pl.BlockSpec((tm, w_), lambda i: (i, 0))
    with_router = router is not None
    in_specs = [row(D_MODEL), row(ATTN_WIDTH), row(RWKV_WIDTH), row(SSM_WIDTH),
                _const_spec((D_MODEL, D_MODEL)), _const_spec((1, D_MODEL))]
    args = [h, attn, rw, ss, w_out, g2.reshape(1, D_MODEL)]
    out_specs = [row(D_MODEL), row(D_MODEL)]
    out_shape = [jax.ShapeDtypeStruct((t, D_MODEL), F32),
                 jax.ShapeDtypeStruct((t, D_MODEL), F32 if with_router else BF16)]
    if with_router:
        rt = jnp.zeros((D_MODEL, ROUTER_LANES), F32).at[:, :N_EXPERTS].set(router.astype(F32))
        in_specs.append(_const_spec((D_MODEL, ROUTER_LANES)))
        args.append(rt)
        out_specs += [row(ROUTER_LANES), row(ROUTER_LANES)]
        out_shape += [jax.ShapeDtypeStruct((t, ROUTER_LANES), jnp.int32),
                      jax.ShapeDtypeStruct((t, ROUTER_LANES), F32)]
    return pl.pallas_call(
        functools.partial(_out_proj_kernel, with_router=with_router),
        grid=(t // tm,),
        in_specs=in_specs, out_specs=out_specs, out_shape=out_shape,
        compiler_params=_cparams(("arbitrary",)),
        name="out_proj_router" if with_router else "out_proj",
    )(*args)


def _ffn_kernel(h_ref, hn_ref, w1_ref, w3_ref, w2_ref, o_ref):
    f = pl.program_id(1)

    @pl.when(f == 0)
    def _():
        o_ref[...] = h_ref[...]

    hn = hn_ref[...]
    a1 = jnp.dot(hn, w1_ref[...], preferred_element_type=F32)
    a3 = jnp.dot(hn, w3_ref[...], preferred_element_type=F32)
    act = (a1 * jax.nn.sigmoid(a1) * a3).astype(BF16)
    o_ref[...] += jnp.dot(act, w2_ref[...], preferred_element_type=F32)


def _ffn(h, hn, w1, w3, w2):
    t = h.shape[0]
    tm = min(ROW_TILE, t)
    d_ff = w1.shape[1]
    tf = FFN_COL_TILE
    return pl.pallas_call(
        _ffn_kernel,
        grid=(t // tm, d_ff // tf),
        in_specs=[pl.BlockSpec((tm, D_MODEL), lambda i, f: (i, 0)),
                  pl.BlockSpec((tm, D_MODEL), lambda i, f: (i, 0)),
                  pl.BlockSpec((D_MODEL, tf), lambda i, f: (0, f)),
                  pl.BlockSpec((D_MODEL, tf), lambda i, f: (0, f)),
                  pl.BlockSpec((tf, D_MODEL), lambda i, f: (f, 0))],
        out_specs=pl.BlockSpec((tm, D_MODEL), lambda i, f: (i, 0)),
        out_shape=jax.ShapeDtypeStruct((t, D_MODEL), F32),
        compiler_params=_cparams(("arbitrary", "arbitrary")),
        name="ffn",
    )(h, hn, w1, w3, w2)


def _gather_rows_kernel(idx_ref, src_ref, o_ref, sem):
    rows = o_ref.shape[0]

    def copy(r):
        return pltpu.make_async_copy(src_ref.at[pl.ds(idx_ref[0, r], 1)], o_ref.at[pl.ds(r, 1)], sem)

    def issue(r, c):
        copy(r).start()
        return c

    def drain(r, c):
        copy(r).wait()
        return c

    lax.fori_loop(0, rows, issue, 0)
    lax.fori_loop(0, rows, drain, 0)


def _gather_rows(src, idx):
    n = idx.shape[0]
    tg = min(GATHER_TILE, n)
    width = src.shape[1]
    return pl.pallas_call(
        _gather_rows_kernel,
        grid=(n // tg,),
        in_specs=[pl.BlockSpec((None, 1, tg), lambda i: (i, 0, 0), memory_space=pltpu.SMEM),
                  pl.BlockSpec(memory_space=pl.ANY)],
        out_specs=pl.BlockSpec((tg, width), lambda i: (i, 0)),
        out_shape=jax.ShapeDtypeStruct((n, width), src.dtype),
        scratch_shapes=[pltpu.SemaphoreType.DMA(())],
        compiler_params=_cparams(("arbitrary",)),
        name="moe_gather",
    )(idx.reshape(n // tg, 1, tg), src)


def _expert_kernel(te_ref, tv_ref, x_ref, w1_ref, w3_ref, w2_ref, o_ref, xb_ref):
    i = pl.program_id(0)
    f = pl.program_id(1)

    @pl.when(f == 0)
    def _():
        xb_ref[...] = x_ref[...].astype(BF16)
        o_ref[...] = jnp.zeros_like(o_ref)

    @pl.when(tv_ref[i] > 0)
    def _():
        xb = xb_ref[...]
        a1 = jnp.dot(xb, w1_ref[...], preferred_element_type=F32)
        a3 = jnp.dot(xb, w3_ref[...], preferred_element_type=F32)
        act = (a1 * jax.nn.sigmoid(a1) * a3).astype(BF16)
        o_ref[...] += jnp.dot(act, w2_ref[...], preferred_element_type=F32)


def _experts(xs, tile_expert, tile_valid, w1, w3, w2, tm):
    n = xs.shape[0]
    d_ff = w1.shape[2]
    tf = FFN_COL_TILE
    nf = d_ff // tf
    fidx = lambda i, f, tv: jnp.where(tv[i] > 0, f, nf - 1)
    grid_spec = pltpu.PrefetchScalarGridSpec(
        num_scalar_prefetch=2,
        grid=(n // tm, nf),
        in_specs=[pl.BlockSpec((tm, D_MODEL), lambda i, f, te, tv: (i, 0)),
                  pl.BlockSpec((None, D_MODEL, tf), lambda i, f, te, tv: (te[i], 0, fidx(i, f, tv))),
                  pl.BlockSpec((None, D_MODEL, tf), lambda i, f, te, tv: (te[i], 0, fidx(i, f, tv))),
                  pl.BlockSpec((None, tf, D_MODEL), lambda i, f, te, tv: (te[i], fidx(i, f, tv), 0))],
        out_specs=pl.BlockSpec((tm, D_MODEL), lambda i, f, te, tv: (i, 0)),
        scratch_shapes=[pltpu.VMEM((tm, D_MODEL), BF16)],
    )
    return pl.pallas_call(
        _expert_kernel,
        grid_spec=grid_spec,
        out_shape=jax.ShapeDtypeStruct((n, D_MODEL), F32),
        compiler_params=_cparams(("arbitrary", "arbitrary")),
        name="moe_experts",
    )(tile_expert, tile_valid, xs, w1, w3, w2)


def _combine_kernel(p0_ref, p1_ref, h_ref, gate_ref, ys_ref, o_ref, b0, b1, sem0, sem1):
    rows = o_ref.shape[0]

    def copies(r):
        return (pltpu.make_async_copy(ys_ref.at[pl.ds(p0_ref[0, r], 1)], b0.at[pl.ds(r, 1)], sem0),
                pltpu.make_async_copy(ys_ref.at[pl.ds(p1_ref[0, r], 1)], b1.at[pl.ds(r, 1)], sem1))

    def issue(r, c):
        c0, c1 = copies(r)
        c0.start()
        c1.start()
        return c

    def drain(r, c):
        c0, c1 = copies(r)
        c0.wait()
        c1.wait()
        return c

    lax.fori_loop(0, rows, issue, 0)
    lax.fori_loop(0, rows, drain, 0)
    gate = gate_ref[...]
    o_ref[...] = h_ref[...] + gate[:, 0:1] * b0[...] + gate[:, 1:2] * b1[...]


def _combine(h, gates, ys, pos0, pos1):
    t = h.shape[0]
    tg = min(GATHER_TILE, t)
    smem = pl.BlockSpec((None, 1, tg), lambda i: (i, 0, 0), memory_space=pltpu.SMEM)
    return pl.pallas_call(
        _combine_kernel,
        grid=(t // tg,),
        in_specs=[smem, smem,
                  pl.BlockSpec((tg, D_MODEL), lambda i: (i, 0)),
                  pl.BlockSpec((tg, ROUTER_LANES), lambda i: (i, 0)),
                  pl.BlockSpec(memory_space=pl.ANY)],
        out_specs=pl.BlockSpec((tg, D_MODEL), lambda i: (i, 0)),
        out_shape=jax.ShapeDtypeStruct((t, D_MODEL), F32),
        scratch_shapes=[pltpu.VMEM((tg, D_MODEL), F32), pltpu.VMEM((tg, D_MODEL), F32),
                        pltpu.SemaphoreType.DMA(()), pltpu.SemaphoreType.DMA(())],
        compiler_params=_cparams(("arbitrary",)),
        name="moe_combine",
    )(pos0.reshape(t // tg, 1, tg), pos1.reshape(t // tg, 1, tg), h, gates, ys)


def _moe(h, hn, idx, gates, w1, w3, w2):
    t = h.shape[0]
    tm = min(ROW_TILE, t)
    n_tiles = (t * TOP_K) // tm + N_EXPERTS
    n_rows = n_tiles * tm
    flat_e = idx[:, :TOP_K].reshape(-1)
    onehot = (flat_e[:, None] == jnp.arange(N_EXPERTS)[None, :]).astype(jnp.int32)
    rank = jnp.sum((jnp.cumsum(onehot, axis=0) - onehot) * onehot, axis=1)
    counts = jnp.sum(onehot, axis=0)
    padded = ((counts + tm - 1) // tm) * tm
    ends = jnp.cumsum(padded)
    starts = ends - padded
    pos = starts[flat_e] + rank
    token_of = jnp.zeros((n_rows,), jnp.int32).at[pos].set(jnp.arange(t * TOP_K, dtype=jnp.int32) // TOP_K)
    tile_start = jnp.arange(n_tiles, dtype=jnp.int32) * tm
    tile_valid = (tile_start < ends[-1]).astype(jnp.int32)
    tile_expert = jnp.minimum(jnp.sum((tile_start[:, None] >= ends[None, :]).astype(jnp.int32), axis=1),
                              N_EXPERTS - 1)
    last_expert = tile_expert[jnp.maximum(ends[-1] // tm - 1, 0)]
    tile_expert = jnp.where(tile_valid > 0, tile_expert, last_expert).astype(jnp.int32)
    xs = _gather_rows(hn, token_of)
    ys = _experts(xs, tile_expert, tile_valid, w1, w3, w2, tm)
    pos = pos.reshape(t, TOP_K).astype(jnp.int32)
    return _combine(h, gates, ys, pos[:, 0], pos[:, 1])


def _permute_in_cols(w):
    o = V_END
    wd = RWKV_WIDTH
    r = (o, o + wd)
    xw = (r[1], r[1] + RWKV_DECAY_LORA)
    k = (xw[1], xw[1] + wd)
    v = (k[1], k[1] + wd)
    xa = (v[1], v[1] + RWKV_A_LORA)
    xg = (xa[1], xa[1] + RWKV_GATE_LORA)
    order = [(0, o), r, k, v, xg, xw, xa, (RWKV_END, IN_COLS)]
    return jnp.concatenate([w[..., a:b] for a, b in order], axis=-1)


def kernel(x, norm1_g, w_in, q_norm_g, k_norm_g, attn_sinks, rel_bias, rwkv_mu, rwkv_w0, rwkv_w2, rwkv_a0, rwkv_a2, rwkv_g2, rwkv_k_k, rwkv_k_a, rwkv_r_k, rwkv_ln_w, rwkv_ln_b, ssm_lambda_re, ssm_lambda_im, ssm_b_re, ssm_b_im, ssm_c_re, ssm_c_im, ssm_d, ssm_log_dt, ssm_glu_w, ssm_glu_b, attn_out_g, ssm_out_g, w_out, norm2_g, ffn_w1, ffn_w3, ffn_w2, moe_router, moe_w1, moe_w3, moe_w2):
    batch, seq, _ = x.shape
    depth = w_in.shape[0]
    h = x.reshape(batch * seq, D_MODEL)
    for i in range(depth):
        w_in_i = _permute_in_cols(w_in[i]).astype(BF16)
        mu_i = _permute_in_cols(jnp.pad(rwkv_mu[i], (V_END, SSM_WIDTH)))[V_END:RWKV_END]
        q, kv, zr, zs = _in_proj(h, norm1_g[i], w_in_i)
        attn = _attention(q, kv, batch, seq, q_norm_g[i], k_norm_g[i], attn_sinks[i], rel_bias,
                          attn_out_g[i])
        rw = _rwkv(zr, batch, seq, mu_i, rwkv_w0[i], rwkv_w2[i], rwkv_a0[i], rwkv_a2[i], rwkv_g2[i],
                   rwkv_k_k[i], rwkv_k_a[i], rwkv_r_k[i], rwkv_ln_w[i], rwkv_ln_b[i])
        ss = _s5(zs, batch, seq, ssm_lambda_re[i], ssm_lambda_im[i], ssm_b_re[i], ssm_b_im[i],
                 ssm_c_re[i], ssm_c_im[i], ssm_d[i], ssm_log_dt[i], ssm_glu_w[i], ssm_glu_b[i],
                 ssm_out_g[i])
        j = i // 2
        if i % 2 == 0:
            h, hn = _out_proj(h, attn, rw, ss, w_out[i].astype(BF16), norm2_g[i])
            h = _ffn(h, hn, ffn_w1[j].astype(BF16), ffn_w3[j].astype(BF16), ffn_w2[j].astype(BF16))
        else:
            h, hn, idx, gates = _out_proj(h, attn, rw, ss, w_out[i].astype(BF16), norm2_g[i],
                                          router=moe_router[j])
            h = _moe(h, hn, idx, gates, moe_w1[j].astype(BF16), moe_w3[j].astype(BF16),
                     moe_w2[j].astype(BF16))
    return h.reshape(batch, seq, D_MODEL)
```

```python
import functools
import math

import numpy as np
import jax
import jax.numpy as jnp
from jax import lax
from jax.experimental import pallas as pl
from jax.experimental.pallas import tpu as pltpu

F32 = jnp.float32
BF16 = jnp.bfloat16

D_MODEL = 2048
HEAD_DIM = 64
ATTN_WIDTH = 1024
RWKV_WIDTH = 512
SSM_WIDTH = 512
ATTN_HEADS = 16
ATTN_KV_HEADS = 4
ATTN_GROUP = 4
KV_WIDTH = 256
WINDOW = 128
BLOCK = 128
REL_BUCKETS = 32
REL_MAX_DISTANCE = 128
RWKV_HEADS = 8
RWKV_DECAY_LORA = 64
RWKV_A_LORA = 64
RWKV_GATE_LORA = 128
RWKV_COLS = 1792
RWKV_DECAY_SCALE = math.exp(-0.5)
RWKV_LN_EPS = HEAD_DIM * 1e-5
SSM_GROUP_CH = 16
SSM_GROUPS = 32
SSM_STATE = 64
SSM_STATES = SSM_GROUPS * SSM_STATE
Q_END = ATTN_WIDTH
K_END = Q_END + KV_WIDTH
V_END = K_END + KV_WIDTH
RWKV_END = V_END + RWKV_COLS
IN_COLS = RWKV_END + SSM_WIDTH
D_FF = 5632
N_EXPERTS = 8
TOP_K = 2
D_FF_EXPERT = 7168
NORM_EPS = 1e-6

LANES = 128
MXU_DIM = 256
VMEM_LIMIT = 56 * 1024 * 1024

ROW_TILE = 512
FFN_COL_TILE = 512
SCAN_TILE = 256
S5_SEGMENTS = 8
S5_STRIP = 512
RWKV_CHUNK = 64
RWKV_QUAD = 4
GATHER_TILE = 256
ROUTER_LANES = 128


def _cparams(sem):
    return pltpu.CompilerParams(dimension_semantics=sem, vmem_limit_bytes=VMEM_LIMIT)


def _mm(a, b):
    return jnp.dot(a.astype(BF16), b.astype(BF16), preferred_element_type=F32)


def _mm_nt(a, b):
    return lax.dot_general(a.astype(BF16), b.astype(BF16), (((1,), (1,)), ((), ())),
                           preferred_element_type=F32)


def _mm_tn(a, b):
    return lax.dot_general(a.astype(BF16), b.astype(BF16), (((0,), (0,)), ((), ())),
                           preferred_element_type=F32)


def _split2(x):
    hi = x.astype(BF16)
    lo = (x - hi.astype(F32)).astype(BF16)
    return hi, lo


def _split3(x):
    h1 = x.astype(BF16)
    r1 = x - h1.astype(F32)
    h2 = r1.astype(BF16)
    h3 = (r1 - h2.astype(F32)).astype(BF16)
    return h1, h2, h3


def _seg_sum(x, ones_bd):
    hi, lo = _split2(x)
    return (jnp.dot(hi, ones_bd, preferred_element_type=F32)
            + jnp.dot(lo, ones_bd, preferred_element_type=F32))


def _const_spec(shape):
    nd = len(shape)
    return pl.BlockSpec(shape, lambda *_: (0,) * nd, pipeline_mode=pl.Buffered(1))


def _in_proj_kernel(x_ref, g_ref, w_ref, q_ref, kv_ref, rw_ref, ss_ref):
    x = x_ref[...]
    ms = jnp.mean(x * x, axis=-1, keepdims=True)
    hn = (x * lax.rsqrt(ms + NORM_EPS) * g_ref[...]).astype(BF16)
    q_ref[...] = jnp.dot(hn, w_ref[:, 0:Q_END], preferred_element_type=F32)
    kv_ref[...] = jnp.dot(hn, w_ref[:, Q_END:V_END], preferred_element_type=F32)
    rw_ref[...] = jnp.dot(hn, w_ref[:, V_END:RWKV_END], preferred_element_type=F32)
    ss_ref[...] = jnp.dot(hn, w_ref[:, RWKV_END:IN_COLS], preferred_element_type=F32)


def _in_proj(h, g, w):
    t = h.shape[0]
    tm = min(ROW_TILE, t)
    row = lambda w_: pl.BlockSpec((tm, w_), lambda i: (i, 0))
    return pl.pallas_call(
        _in_proj_kernel,
        grid=(t // tm,),
        in_specs=[row(D_MODEL), _const_spec((1, D_MODEL)), _const_spec((D_MODEL, IN_COLS))],
        out_specs=[row(ATTN_WIDTH), row(2 * KV_WIDTH), row(RWKV_COLS), row(SSM_WIDTH)],
        out_shape=[jax.ShapeDtypeStruct((t, ATTN_WIDTH), F32),
                   jax.ShapeDtypeStruct((t, 2 * KV_WIDTH), F32),
                   jax.ShapeDtypeStruct((t, RWKV_COLS), F32),
                   jax.ShapeDtypeStruct((t, SSM_WIDTH), F32)],
        compiler_params=_cparams(("arbitrary",)),
        name="in_proj",
    )(h, g.reshape(1, D_MODEL), w)


def _attn_kernel(sink_ref, q_ref, kvp_ref, kvc_ref, bias_ref, qg_ref, kg_ref, og_ref, e_ref,
                 o_ref, acc_ref):
    first = pl.program_id(1) == 0
    e = e_ref[...]
    kv = jnp.concatenate([kvp_ref[...], kvc_ref[...]], axis=0)

    def head_norm(t, g):
        ss = _seg_sum(t * t, e)
        return t * lax.rsqrt(ss * (1.0 / HEAD_DIM) + NORM_EPS) * g

    kn = head_norm(kv[:, :KV_WIDTH], kg_ref[...]).astype(BF16)
    vb = kv[:, KV_WIDTH:].astype(BF16)
    key_idx = lax.broadcasted_iota(jnp.int32, (BLOCK, 2 * BLOCK), 1)
    no_prev = jnp.logical_and(first, key_idx < BLOCK)
    scale = HEAD_DIM ** -0.5
    for h in range(ATTN_KV_HEADS):
        qn = (head_norm(q_ref[:, h * MXU_DIM:(h + 1) * MXU_DIM], qg_ref[...]) * scale).astype(BF16)
        kh = kn[:, h * HEAD_DIM:(h + 1) * HEAD_DIM]
        vh = vb[:, h * HEAD_DIM:(h + 1) * HEAD_DIM]
        for g in range(ATTN_GROUP):
            head = h * ATTN_GROUP + g
            s = _mm_nt(qn[:, g * HEAD_DIM:(g + 1) * HEAD_DIM], kh) + bias_ref[head]
            s = jnp.where(no_prev, -jnp.inf, s)
            sink = sink_ref[head]
            m = jnp.maximum(jnp.max(s, axis=-1, keepdims=True), sink)
            p = jnp.exp(s - m)
            denom = jnp.sum(p, axis=-1, keepdims=True) + jnp.exp(sink - m)
            o = jnp.dot(p.astype(BF16), vh, preferred_element_type=F32) / denom
            acc_ref[:, head * HEAD_DIM:(head + 1) * HEAD_DIM] = o
    o = acc_ref[...]
    ms = jnp.mean(o * o, axis=-1, keepdims=True)
    o_ref[...] = (o * lax.rsqrt(ms + NORM_EPS) * og_ref[...]).astype(BF16)


def _t5_bucket_table():
    qi = np.arange(BLOCK)[:, None]
    kj = np.arange(2 * BLOCK)[None, :]
    dist = BLOCK + qi - kj
    max_exact = REL_BUCKETS // 2
    d = np.maximum(dist, 0)
    large = max_exact + (np.log(np.maximum(d, 1).astype(np.float32) / max_exact)
                         / math.log(REL_MAX_DISTANCE / max_exact)
                         * (REL_BUCKETS - max_exact)).astype(np.int32)
    large = np.minimum(large, REL_BUCKETS - 1)
    bucket = np.where(d < max_exact, d, large)
    valid = (dist >= 0) & (dist < WINDOW)
    return bucket, valid


def _attention(q, kv, batch, seq, q_g, k_g, sinks, rel_bias, out_g):
    nb = seq // BLOCK
    bucket, valid = _t5_bucket_table()
    bias = jnp.transpose(rel_bias.astype(F32)[bucket], (2, 0, 1))
    bias = jnp.where(valid[None], bias, -jnp.inf)
    seg = np.arange(MXU_DIM) // HEAD_DIM
    ones_bd = jnp.asarray(seg[:, None] == seg[None, :], dtype=BF16)
    tile4 = lambda g: jnp.tile(g.astype(F32), ATTN_GROUP).reshape(1, MXU_DIM)
    grid_spec = pltpu.PrefetchScalarGridSpec(
        num_scalar_prefetch=1,
        grid=(batch, nb),
        in_specs=[
            pl.BlockSpec((BLOCK, ATTN_WIDTH), lambda b, n, s: (b * nb + n, 0)),
            pl.BlockSpec((BLOCK, 2 * KV_WIDTH), lambda b, n, s: (jnp.maximum(b * nb + n - 1, 0), 0)),
            pl.BlockSpec((BLOCK, 2 * KV_WIDTH), lambda b, n, s: (b * nb + n, 0)),
            pl.BlockSpec((ATTN_HEADS, BLOCK, 2 * BLOCK), lambda b, n, s: (0, 0, 0)),
            pl.BlockSpec((1, MXU_DIM), lambda b, n, s: (0, 0)),
            pl.BlockSpec((1, MXU_DIM), lambda b, n, s: (0, 0)),
            pl.BlockSpec((1, ATTN_WIDTH), lambda b, n, s: (0, 0)),
            pl.BlockSpec((MXU_DIM, MXU_DIM), lambda b, n, s: (0, 0)),
        ],
        out_specs=pl.BlockSpec((BLOCK, ATTN_WIDTH), lambda b, n, s: (b * nb + n, 0)),
        scratch_shapes=[pltpu.VMEM((BLOCK, ATTN_WIDTH), F32)],
    )
    return pl.pallas_call(
        _attn_kernel,
        grid_spec=grid_spec,
        out_shape=jax.ShapeDtypeStruct((batch * seq, ATTN_WIDTH), BF16),
        compiler_params=_cparams(("arbitrary", "arbitrary")),
        name="swa",
    )(sinks.astype(F32), q, kv, kv, bias, tile4(q_g), tile4(k_g), out_g.reshape(1, ATTN_WIDTH), ones_bd)


def _rwkv_consts():
    c, g_heads = RWKV_CHUNK, RWKV_QUAD
    w = g_heads * HEAD_DIM
    row = lax.broadcasted_iota(jnp.int32, (c, g_heads * c), 0)
    col = lax.broadcasted_iota(jnp.int32, (c, g_heads * c), 1) % c
    strict = col < row
    eye = (row == col).astype(F32)
    diag2 = jnp.logical_and((row // 2) == (col // 2), strict)
    levels = []
    bs = 2
    while bs < c:
        levels.append(((row // (2 * bs)) == (col // (2 * bs))) & ((row // bs) % 2 == 1)
                      & ((col // bs) % 2 == 0))
        bs *= 2
    hrow = lax.broadcasted_iota(jnp.int32, (w, w), 0) // HEAD_DIM
    hcol = lax.broadcasted_iota(jnp.int32, (w, w), 1) // HEAD_DIM
    return strict, col <= row, eye, diag2, levels, hrow == hcol


def _rwkv_stack(x, bm):
    return jnp.tile(x.astype(BF16), (RWKV_QUAD, 1)) * bm


def _rwkv_chunk_prepare(r, lw, cum, k, v, a, b, bm, consts):
    strict, incl, eye, diag2, levels, _ = consts
    c = RWKV_CHUNK
    gc = RWKV_QUAD * c
    last = cum[c - 1:c]
    p_inv = jnp.exp(-cum)
    ratio = jnp.exp(last - cum)
    ar = jnp.concatenate([a * jnp.exp(cum - lw), r * jnp.exp(cum)], axis=0).astype(BF16)
    bks = jnp.concatenate([_rwkv_stack(b * p_inv, bm), _rwkv_stack(k * p_inv, bm)], axis=0)
    big = lax.dot_general(ar, bks, (((1,), (1,)), ((), ())), preferred_element_type=F32)
    n_cat = jnp.where(strict, big[:c, :gc], 0.0)
    a_k = jnp.where(strict, big[:c, gc:], 0.0)
    r_b = jnp.where(incl, big[c:, :gc], 0.0)
    r_k = jnp.where(incl, big[c:, gc:], 0.0)
    t_inv = eye + jnp.where(diag2, n_cat, 0.0)
    for lm in levels:
        tn = _mm(t_inv, _rwkv_stack(jnp.where(lm, n_cat, 0.0), bm))
        t_inv = t_inv + _mm(tn, _rwkv_stack(t_inv, bm))
    v_stack = _rwkv_stack(v, bm)
    return dict(ar=ar, t_inv=t_inv.astype(BF16), r_b=r_b.astype(BF16), w0c=_mm(a_k, v_stack),
                yc=_mm(r_k, v_stack), v=v,
                uv_rhs=jnp.concatenate([b * ratio, k * ratio], axis=0).astype(BF16),
                p_last=jnp.exp(last))


def _rwkv_chunk_apply(d, ht, bm, consts):
    c = RWKV_CHUNK
    ah = lax.dot_general(d["ar"], ht.astype(BF16), (((1,), (1,)), ((), ())),
                         preferred_element_type=F32)
    u = _mm(d["t_inv"], _rwkv_stack(ah[:c] + d["w0c"], bm))
    y = ah[c:] + _mm(d["r_b"], _rwkv_stack(u, bm)) + d["yc"]
    upd = _mm_tn(jnp.concatenate([u, d["v"]], axis=0), d["uv_rhs"])
    return y, ht * d["p_last"] + jnp.where(consts[5], upd, 0.0)


def _rwkv_kernel(z_ref, mu_ref, w2a_ref, g2_ref, w0_ref, a0_ref, kk_ref, ka_ref, rk_ref, lnw_ref,
                 lnb_ref, e_ref, tri_ref, bm_ref, o_ref, carry_ref, y_s, h_s):
    tb = z_ref.shape[0]
    wd = RWKV_WIDTH
    quad_w = RWKV_QUAD * HEAD_DIM
    n_quads = RWKV_HEADS // RWKV_QUAD
    c = RWKV_CHUNK

    @pl.when(pl.program_id(1) == 0)
    def _():
        carry_ref[...] = jnp.zeros_like(carry_ref)
        h_s[...] = jnp.zeros_like(h_s)

    z = z_ref[...]
    rows = lax.broadcasted_iota(jnp.int32, (tb, 1), 0)
    prev = jnp.where(rows == 0, carry_ref[...], pltpu.roll(z, 1, axis=0))
    carry_ref[...] = z[tb - 1:tb]
    zs = z + mu_ref[...] * (prev - z)
    r = zs[:, 0:wd]
    k = zs[:, wd:2 * wd]
    v = zs[:, 2 * wd:3 * wd]
    xg = zs[:, 3 * wd:3 * wd + RWKV_GATE_LORA]
    xwa = zs[:, 3 * wd + RWKV_GATE_LORA:]
    lane = lax.broadcasted_iota(jnp.int32, (1, RWKV_DECAY_LORA + RWKV_A_LORA), 1)
    lora_in = jnp.where(lane < RWKV_DECAY_LORA, jnp.tanh(xwa), xwa)
    lora = _mm(lora_in, w2a_ref[...])
    lw = -RWKV_DECAY_SCALE * jax.nn.sigmoid(w0_ref[...] + lora[:, :wd])
    iclr = jax.nn.sigmoid(a0_ref[...] + lora[:, wd:])
    gate = _mm(jax.nn.sigmoid(xg), g2_ref[...])
    kk = k * kk_ref[...]
    e = e_ref[...]
    seg = lambda x: jnp.concatenate(
        [_seg_sum(x[:, q * quad_w:(q + 1) * quad_w], e) for q in range(n_quads)], axis=1)
    kk = kk * lax.rsqrt(jnp.maximum(seg(kk * kk), 1e-12))
    k2 = k * (1.0 + (iclr - 1.0) * ka_ref[...])
    l1, l2, l3 = _split3(lw)
    tri = tri_ref[...]
    cum = (jnp.dot(tri, l1, preferred_element_type=F32) + jnp.dot(tri, l2, preferred_element_type=F32)
           + jnp.dot(tri, l3, preferred_element_type=F32))
    a = -kk
    b = kk * iclr

    consts = _rwkv_consts()
    bm = bm_ref[...]
    n_chunks = tb // c
    prepared = {}
    for ci in range(n_chunks):
        for q in range(n_quads):
            sl = (slice(ci * c, (ci + 1) * c), slice(q * quad_w, (q + 1) * quad_w))
            prepared[ci, q] = _rwkv_chunk_prepare(r[sl], lw[sl], cum[sl], k2[sl], v[sl], a[sl], b[sl],
                                                  bm, consts)
    for q in range(n_quads):
        ht = h_s[q]
        for ci in range(n_chunks):
            y_c, ht = _rwkv_chunk_apply(prepared[ci, q], ht, bm, consts)
            y_s[ci * c:(ci + 1) * c, q * quad_w:(q + 1) * quad_w] = y_c
        h_s[q] = ht

    y = y_s[...]
    mean = seg(y) * (1.0 / HEAD_DIM)
    d = y - mean
    var = seg(d * d) * (1.0 / HEAD_DIM)
    yn = d * lax.rsqrt(var + RWKV_LN_EPS) * lnw_ref[...] + lnb_ref[...]
    bonus = seg(r * k2 * rk_ref[...]) * v
    o_ref[...] = ((yn + bonus) * gate).astype(BF16)


def _rwkv(z, batch, seq, mu, w0, w2, a0, a2, g2, k_k, k_a, r_k, ln_w, ln_b):
    tb = min(SCAN_TILE, seq)
    nb = seq // tb
    wd = RWKV_WIDTH
    quad_w = RWKV_QUAD * HEAD_DIM
    w2a = jnp.zeros((RWKV_DECAY_LORA + RWKV_A_LORA, 2 * wd), F32)
    w2a = w2a.at[:RWKV_DECAY_LORA, :wd].set(w2).at[RWKV_DECAY_LORA:, wd:].set(a2).astype(BF16)
    seg = np.arange(quad_w) // HEAD_DIM
    ones_bd = jnp.asarray(seg[:, None] == seg[None, :], dtype=BF16)
    t_idx = np.arange(tb)
    tri = jnp.asarray((t_idx[:, None] // RWKV_CHUNK == t_idx[None, :] // RWKV_CHUNK)
                      & (t_idx[None, :] <= t_idx[:, None]), dtype=BF16)
    stack_rows = np.arange(RWKV_QUAD * RWKV_CHUNK) // RWKV_CHUNK
    stack_mask = jnp.asarray(stack_rows[:, None] == seg[None, :], dtype=BF16)
    vec = lambda p: p.astype(F32).reshape(1, -1)
    return pl.pallas_call(
        _rwkv_kernel,
        grid=(batch, nb),
        in_specs=[pl.BlockSpec((tb, RWKV_COLS), lambda b, j: (b * nb + j, 0)),
                  _const_spec((1, RWKV_COLS)),
                  _const_spec((RWKV_DECAY_LORA + RWKV_A_LORA, 2 * wd)),
                  _const_spec((RWKV_GATE_LORA, wd))]
                 + [_const_spec((1, wd))] * 7
                 + [_const_spec((quad_w, quad_w)), _const_spec((tb, tb)),
                    _const_spec((RWKV_QUAD * RWKV_CHUNK, quad_w))],
        out_specs=pl.BlockSpec((tb, wd), lambda b, j: (b * nb + j, 0)),
        out_shape=jax.ShapeDtypeStruct((batch * seq, wd), BF16),
        scratch_shapes=[pltpu.VMEM((1, RWKV_COLS), F32), pltpu.VMEM((tb, wd), F32),
                        pltpu.VMEM((RWKV_HEADS // RWKV_QUAD, quad_w, quad_w), F32)],
        compiler_params=_cparams(("arbitrary", "arbitrary")),
        name="rwkv7",
    )(z, vec(mu), w2a, g2.astype(BF16), vec(w0), vec(a0), vec(k_k), vec(k_a), vec(r_k), vec(ln_w),
      vec(ln_b), ones_bd, tri, stack_mask)


def _cmul_add(ar, ai, xr, xi, br, bi):
    return ar * xr - ai * xi + br, ar * xi + ai * xr + bi


def _s5_kernel(u_ref, perm_ref, bbr_ref, bbi_ref, cr_ref, ci_ref, a1_ref, aseg_ref, apj_ref, ask_ref,
               d_ref, gw_ref, gb_ref, og_ref, o_ref, xr_s, xi_s, car_s, cai_s):
    tb = u_ref.shape[0]
    seg_len = tb // S5_SEGMENTS
    strip = S5_STRIP
    n_strips = SSM_STATES // strip
    n_groups = SSM_WIDTH // LANES

    @pl.when(pl.program_id(1) == 0)
    def _():
        car_s[...] = jnp.zeros_like(car_s)
        cai_s[...] = jnp.zeros_like(cai_s)

    perm = perm_ref[...]
    u1, u2, u3 = _split3(u_ref[...])
    u = (jnp.dot(perm, u1, preferred_element_type=F32) + jnp.dot(perm, u2, preferred_element_type=F32)
         + jnp.dot(perm, u3, preferred_element_type=F32))
    ub = u.astype(BF16)
    sw = SSM_STATES // n_groups
    for g in range(n_groups):
        ug = ub[:, g * LANES:(g + 1) * LANES]
        xr_s[:, g * sw:(g + 1) * sw] = jnp.dot(ug, bbr_ref[g * LANES:(g + 1) * LANES, g * sw:(g + 1) * sw],
                                               preferred_element_type=F32)
        xi_s[:, g * sw:(g + 1) * sw] = jnp.dot(ug, bbi_ref[g * LANES:(g + 1) * LANES, g * sw:(g + 1) * sw],
                                               preferred_element_type=F32)

    for s in range(n_strips):
        cols = slice(s * strip, (s + 1) * strip)
        ar = jnp.broadcast_to(a1_ref[0:1, cols], (S5_SEGMENTS, strip))
        ai = jnp.broadcast_to(a1_ref[1:2, cols], (S5_SEGMENTS, strip))

        def scan_body(j, x):
            rs = pl.ds(pl.multiple_of(j * S5_SEGMENTS, S5_SEGMENTS), S5_SEGMENTS)
            nr, ni = _cmul_add(ar, ai, x[0], x[1], xr_s[rs, cols], xi_s[rs, cols])
            xr_s[rs, cols] = nr
            xi_s[rs, cols] = ni
            return nr, ni

        fr, fi = lax.fori_loop(1, seg_len, scan_body,
                               (xr_s[0:S5_SEGMENTS, cols], xi_s[0:S5_SEGMENTS, cols]), unroll=2)
        seg_row = lax.broadcasted_iota(jnp.int32, (S5_SEGMENTS, strip), 0)
        ir, ii = fr, fi
        for lvl in range(int(math.log2(S5_SEGMENTS))):
            sh = 1 << lvl
            pr = jnp.where(seg_row >= sh, pltpu.roll(ir, sh, axis=0), 0.0)
            pi = jnp.where(seg_row >= sh, pltpu.roll(ii, sh, axis=0), 0.0)
            ir, ii = _cmul_add(aseg_ref[2 * lvl:2 * lvl + 1, cols], aseg_ref[2 * lvl + 1:2 * lvl + 2, cols],
                               pr, pi, ir, ii)
        c_in_r = car_s[:, cols]
        c_in_i = cai_s[:, cols]
        er = jnp.where(seg_row >= 1, pltpu.roll(ir, 1, axis=0), 0.0)
        ei = jnp.where(seg_row >= 1, pltpu.roll(ii, 1, axis=0), 0.0)
        cr_, ci_ = _cmul_add(ask_ref[0:S5_SEGMENTS, cols], ask_ref[S5_SEGMENTS:2 * S5_SEGMENTS, cols],
                             c_in_r, c_in_i, er, ei)
        last = S5_SEGMENTS - 1
        nr, ni = _cmul_add(aseg_ref[0:1, cols], aseg_ref[1:2, cols],
                           cr_[last:last + 1], ci_[last:last + 1], fr[last:last + 1], fi[last:last + 1])
        car_s[:, cols] = nr
        cai_s[:, cols] = ni

        def fix_body(j, carry):
            rs = pl.ds(pl.multiple_of(j * S5_SEGMENTS, S5_SEGMENTS), S5_SEGMENTS)
            pr = apj_ref[pl.ds(2 * j, 1), cols]
            pi = apj_ref[pl.ds(2 * j + 1, 1), cols]
            nr, ni = _cmul_add(pr, pi, cr_, ci_, xr_s[rs, cols], xi_s[rs, cols])
            xr_s[rs, cols] = nr
            xi_s[rs, cols] = ni
            return carry

        lax.fori_loop(0, seg_len, fix_body, 0, unroll=2)

    sg = SSM_STATES // n_groups
    ys = []
    for g in range(n_groups):
        xr = xr_s[:, g * sg:(g + 1) * sg].astype(BF16)
        xi = xi_s[:, g * sg:(g + 1) * sg].astype(BF16)
        ys.append(jnp.dot(xr, cr_ref[g * sg:(g + 1) * sg, g * LANES:(g + 1) * LANES], preferred_element_type=F32)
                  - jnp.dot(xi, ci_ref[g * sg:(g + 1) * sg, g * LANES:(g + 1) * LANES], preferred_element_type=F32))
    y = jnp.concatenate(ys, axis=1) + d_ref[...] * u
    zg = jax.nn.gelu(y)
    gl = _mm(zg, gw_ref[...]) + gb_ref[...]
    out = zg * jax.nn.sigmoid(gl)
    ms = jnp.mean(out * out, axis=-1, keepdims=True)
    out = (out * lax.rsqrt(ms + NORM_EPS) * og_ref[...]).astype(BF16)
    o_ref[...] = lax.dot_general(perm, out, (((0,), (0,)), ((), ())),
                                 preferred_element_type=F32).astype(BF16)


def _s5(u, batch, seq, lam_re, lam_im, b_re, b_im, c_re, c_im, d, log_dt, glu_w, glu_b, out_g):
    tb = min(SCAN_TILE, seq)
    nb = seq // tb
    seg_len = tb // S5_SEGMENTS
    f = lambda p: p.astype(F32)
    lr, li = f(lam_re), f(lam_im)
    dt = jnp.exp(f(log_dt))[:, None]
    mag = jnp.exp(lr * dt)
    ab_re = mag * jnp.cos(li * dt)
    ab_im = mag * jnp.sin(li * dt)
    den = lr * lr + li * li
    fr = ((ab_re - 1.0) * lr + ab_im * li) / den
    fi = (ab_im * lr - (ab_re - 1.0) * li) / den
    bb_re = fr[..., None] * f(b_re) - fi[..., None] * f(b_im)
    bb_im = fr[..., None] * f(b_im) + fi[..., None] * f(b_re)
    eye = jnp.eye(SSM_GROUPS, dtype=F32)
    in_bd = lambda t: jnp.einsum('gpm,gh->gmhp', t, eye).reshape(SSM_WIDTH, SSM_STATES).astype(BF16)
    out_bd = lambda t: jnp.einsum('gmp,gh->gphm', f(t), eye).reshape(SSM_STATES, SSM_WIDTH).astype(BF16)

    def a_pow(n):
        n = jnp.asarray(n, F32)[:, None, None]
        m = jnp.exp(n * (lr * dt))
        ang = n * (li * dt)
        both = jnp.stack([m * jnp.cos(ang), m * jnp.sin(ang)], axis=1)
        return both.reshape(-1, SSM_STATES)

    a1 = a_pow([1])
    aseg = a_pow([seg_len * (1 << l) for l in range(4)])
    apj = a_pow(np.arange(1, seg_len + 1))
    ask_both = a_pow(np.arange(S5_SEGMENTS) * seg_len).reshape(S5_SEGMENTS, 2, SSM_STATES)
    ask = jnp.concatenate([ask_both[:, 0], ask_both[:, 1]], axis=0)
    t_idx = np.arange(tb)
    src_time = (t_idx % S5_SEGMENTS) * seg_len + t_idx // S5_SEGMENTS
    perm = jnp.asarray(src_time[:, None] == t_idx[None, :], dtype=BF16)
    vec = lambda p: f(p).reshape(1, -1)
    buf = lambda: pltpu.VMEM((tb, SSM_STATES), F32)
    return pl.pallas_call(
        _s5_kernel,
        grid=(batch, nb),
        in_specs=[pl.BlockSpec((tb, SSM_WIDTH), lambda b, j: (b * nb + j, 0)),
                  _const_spec((tb, tb)),
                  _const_spec((SSM_WIDTH, SSM_STATES)), _const_spec((SSM_WIDTH, SSM_STATES)),
                  _const_spec((SSM_STATES, SSM_WIDTH)), _const_spec((SSM_STATES, SSM_WIDTH)),
                  _const_spec((2, SSM_STATES)), _const_spec((8, SSM_STATES)),
                  _const_spec((2 * seg_len, SSM_STATES)), _const_spec((2 * S5_SEGMENTS, SSM_STATES)),
                  _const_spec((1, SSM_WIDTH)), _const_spec((SSM_WIDTH, SSM_WIDTH)),
                  _const_spec((1, SSM_WIDTH)), _const_spec((1, SSM_WIDTH))],
        out_specs=pl.BlockSpec((tb, SSM_WIDTH), lambda b, j: (b * nb + j, 0)),
        out_shape=jax.ShapeDtypeStruct((batch * seq, SSM_WIDTH), BF16),
        scratch_shapes=[buf(), buf(),
                        pltpu.VMEM((1, SSM_STATES), F32), pltpu.VMEM((1, SSM_STATES), F32)],
        compiler_params=_cparams(("arbitrary", "arbitrary")),
        name="s5",
    )(u, perm, in_bd(bb_re), in_bd(bb_im), out_bd(c_re), out_bd(c_im), a1, aseg, apj, ask, vec(d),
      glu_w.astype(BF16), vec(glu_b), vec(out_g))


def _out_proj_kernel(h_ref, at_ref, rw_ref, ss_ref, w_ref, g_ref, *rest, with_router):
    if with_router:
        rt_ref, ho_ref, hn_ref, idx_ref, gate_ref = rest
    else:
        ho_ref, hn_ref = rest
    mix = (jnp.dot(at_ref[...], w_ref[0:ATTN_WIDTH, :], preferred_element_type=F32)
           + jnp.dot(rw_ref[...], w_ref[ATTN_WIDTH:ATTN_WIDTH + RWKV_WIDTH, :], preferred_element_type=F32)
           + jnp.dot(ss_ref[...], w_ref[ATTN_WIDTH + RWKV_WIDTH:, :], preferred_element_type=F32))
    h = h_ref[...] + mix
    ho_ref[...] = h
    ms = jnp.mean(h * h, axis=-1, keepdims=True)
    hn = h * lax.rsqrt(ms + NORM_EPS) * g_ref[...]
    if not with_router:
        hn_ref[...] = hn.astype(BF16)
        return
    hn_ref[...] = hn
    h_hi, h_lo = _split2(hn)
    r_hi, r_lo = _split2(rt_ref[...])
    logits = (jnp.dot(h_hi, r_hi, preferred_element_type=F32)
              + jnp.dot(h_lo, r_hi, preferred_element_type=F32)
              + jnp.dot(h_hi, r_lo, preferred_element_type=F32))
    lane_i = lax.broadcasted_iota(jnp.int32, logits.shape, 1)
    lane = lane_i.astype(F32)
    logits = jnp.where(lane_i < N_EXPERTS, logits, -jnp.inf)
    m1 = jnp.max(logits, axis=-1, keepdims=True)
    i1 = jnp.min(jnp.where(logits == m1, lane, float(ROUTER_LANES)), axis=-1, keepdims=True)
    rest_l = jnp.where(lane == i1, -jnp.inf, logits)
    m2 = jnp.max(rest_l, axis=-1, keepdims=True)
    i2 = jnp.min(jnp.where(rest_l == m2, lane, float(ROUTER_LANES)), axis=-1, keepdims=True)
    e2 = jnp.exp(m2 - m1)
    g1 = 1.0 / (1.0 + e2)
    g2 = e2 / (1.0 + e2)
    idx_ref[...] = jnp.where(lane_i == 0, i1, jnp.where(lane_i == 1, i2, 0.0)).astype(jnp.int32)
    gate_ref[...] = jnp.where(lane_i == 0, g1, jnp.where(lane_i == 1, g2, 0.0))


def _out_proj(h, attn, rw, ss, w_out, g2, router=None):
    t = h.shape[0]
    tm = min(ROW_TILE, t)
    row = lambda w_: pl.BlockSpec((tm, w_), lambda i: (i, 0))
    with_router = router is not None
    in_specs = [row(D_MODEL), row(ATTN_WIDTH), row(RWKV_WIDTH), row(SSM_WIDTH),
                _const_spec((D_MODEL, D_MODEL)), _const_spec((1, D_MODEL))]
    args = [h, attn, rw, ss, w_out, g2.reshape(1, D_MODEL)]
    out_specs = [row(D_MODEL), row(D_MODEL)]
    out_shape = [jax.ShapeDtypeStruct((t, D_MODEL), F32),
                 jax.ShapeDtypeStruct((t, D_MODEL), F32 if with_router else BF16)]
    if with_router:
        rt = jnp.zeros((D_MODEL, ROUTER_LANES), F32).at[:, :N_EXPERTS].set(router.astype(F32))
        in_specs.append(_const_spec((D_MODEL, ROUTER_LANES)))
        args.append(rt)
        out_specs += [row(ROUTER_LANES), row(ROUTER_LANES)]
        out_shape += [jax.ShapeDtypeStruct((t, ROUTER_LANES), jnp.int32),
                      jax.ShapeDtypeStruct((t, ROUTER_LANES), F32)]
    return pl.pallas_call(
        functools.partial(_out_proj_kernel, with_router=with_router),
        grid=(t // tm,),
        in_specs=in_specs, out_specs=out_specs, out_shape=out_shape,
        compiler_params=_cparams(("arbitrary",)),
        name="out_proj_router" if with_router else "out_proj",
    )(*args)


def _ffn_kernel(h_ref, hn_ref, w1_ref, w3_ref, w2_ref, o_ref):
    f = pl.program_id(1)

    @pl.when(f == 0)
    def _():
        o_ref[...] = h_ref[...]

    hn = hn_ref[...]
    a1 = jnp.dot(hn, w1_ref[...], preferred_element_type=F32)
    a3 = jnp.dot(hn, w3_ref[...], preferred_element_type=F32)
    act = (a1 * jax.nn.sigmoid(a1) * a3).astype(BF16)
    o_ref[...] += jnp.dot(act, w2_ref[...], preferred_element_type=F32)


def _ffn(h, hn, w1, w3, w2):
    t = h.shape[0]
    tm = min(ROW_TILE, t)
    d_ff = w1.shape[1]
    tf = FFN_COL_TILE
    return pl.pallas_call(
        _ffn_kernel,
        grid=(t // tm, d_ff // tf),
        in_specs=[pl.BlockSpec((tm, D_MODEL), lambda i, f: (i, 0)),
                  pl.BlockSpec((tm, D_MODEL), lambda i, f: (i, 0)),
                  pl.BlockSpec((D_MODEL, tf), lambda i, f: (0, f)),
                  pl.BlockSpec((D_MODEL, tf), lambda i, f: (0, f)),
                  pl.BlockSpec((tf, D_MODEL), lambda i, f: (f, 0))],
        out_specs=pl.BlockSpec((tm, D_MODEL), lambda i, f: (i, 0)),
        out_shape=jax.ShapeDtypeStruct((t, D_MODEL), F32),
        compiler_params=_cparams(("arbitrary", "arbitrary")),
        name="ffn",
    )(h, hn, w1, w3, w2)


def _gather_rows_kernel(idx_ref, src_ref, o_ref, sem):
    rows = o_ref.shape[0]

    def copy(r):
        return pltpu.make_async_copy(src_ref.at[pl.ds(idx_ref[0, r], 1)], o_ref.at[pl.ds(r, 1)], sem)

    def issue(r, c):
        copy(r).start()
        return c

    def drain(r, c):
        copy(r).wait()
        return c

    lax.fori_loop(0, rows, issue, 0)
    lax.fori_loop(0, rows, drain, 0)


def _gather_rows(src, idx):
    n = idx.shape[0]
    tg = min(GATHER_TILE, n)
    width = src.shape[1]
    return pl.pallas_call(
        _gather_rows_kernel,
        grid=(n // tg,),
        in_specs=[pl.BlockSpec((None, 1, tg), lambda i: (i, 0, 0), memory_space=pltpu.SMEM),
                  pl.BlockSpec(memory_space=pl.ANY)],
        out_specs=pl.BlockSpec((tg, width), lambda i: (i, 0)),
        out_shape=jax.ShapeDtypeStruct((n, width), src.dtype),
        scratch_shapes=[pltpu.SemaphoreType.DMA(())],
        compiler_params=_cparams(("arbitrary",)),
        name="moe_gather",
    )(idx.reshape(n // tg, 1, tg), src)


def _expert_kernel(te_ref, tv_ref, x_ref, w1_ref, w3_ref, w2_ref, o_ref, xb_ref):
    i = pl.program_id(0)
    f = pl.program_id(1)

    @pl.when(f == 0)
    def _():
        xb_ref[...] = x_ref[...].astype(BF16)
        o_ref[...] = jnp.zeros_like(o_ref)

    @pl.when(tv_ref[i] > 0)
    def _():
        xb = xb_ref[...]
        a1 = jnp.dot(xb, w1_ref[...], preferred_element_type=F32)
        a3 = jnp.dot(xb, w3_ref[...], preferred_element_type=F32)
        act = (a1 * jax.nn.sigmoid(a1) * a3).astype(BF16)
        o_ref[...] += jnp.dot(act, w2_ref[...], preferred_element_type=F32)


def _experts(xs, tile_expert, tile_valid, w1, w3, w2, tm):
    n = xs.shape[0]
    d_ff = w1.shape[2]
    tf = FFN_COL_TILE
    nf = d_ff // tf
    fidx = lambda i, f, tv: jnp.where(tv[i] > 0, f, nf - 1)
    grid_spec = pltpu.PrefetchScalarGridSpec(
        num_scalar_prefetch=2,
        grid=(n // tm, nf),
        in_specs=[pl.BlockSpec((tm, D_MODEL), lambda i, f, te, tv: (i, 0)),
                  pl.BlockSpec((None, D_MODEL, tf), lambda i, f, te, tv: (te[i], 0, fidx(i, f, tv))),
                  pl.BlockSpec((None, D_MODEL, tf), lambda i, f, te, tv: (te[i], 0, fidx(i, f, tv))),
                  pl.BlockSpec((None, tf, D_MODEL), lambda i, f, te, tv: (te[i], fidx(i, f, tv), 0))],
        out_specs=pl.BlockSpec((tm, D_MODEL), lambda i, f, te, tv: (i, 0)),
        scratch_shapes=[pltpu.VMEM((tm, D_MODEL), BF16)],
    )
    return pl.pallas_call(
        _expert_kernel,
        grid_spec=grid_spec,
        out_shape=jax.ShapeDtypeStruct((n, D_MODEL), F32),
        compiler_params=_cparams(("arbitrary", "arbitrary")),
        name="moe_experts",
    )(tile_expert, tile_valid, xs, w1, w3, w2)


def _combine_kernel(p0_ref, p1_ref, h_ref, gate_ref, ys_ref, o_ref, b0, b1, sem0, sem1):
    rows = o_ref.shape[0]

    def copies(r):
        return (pltpu.make_async_copy(ys_ref.at[pl.ds(p0_ref[0, r], 1)], b0.at[pl.ds(r, 1)], sem0),
                pltpu.make_async_copy(ys_ref.at[pl.ds(p1_ref[0, r], 1)], b1.at[pl.ds(r, 1)], sem1))

    def issue(r, c):
        c0, c1 = copies(r)
        c0.start()
        c1.start()
        return c

    def drain(r, c):
        c0, c1 = copies(r)
        c0.wait()
        c1.wait()
        return c

    lax.fori_loop(0, rows, issue, 0)
    lax.fori_loop(0, rows, drain, 0)
    gate = gate_ref[...]
    o_ref[...] = h_ref[...] + gate[:, 0:1] * b0[...] + gate[:, 1:2] * b1[...]


def _combine(h, gates, ys, pos0, pos1):
    t = h.shape[0]
    tg = min(GATHER_TILE, t)
    smem = pl.BlockSpec((None, 1, tg), lambda i: (i, 0, 0), memory_space=pltpu.SMEM)
    return pl.pallas_call(
        _combine_kernel,
        grid=(t // tg,),
        in_specs=[smem, smem,
                  pl.BlockSpec((tg, D_MODEL), lambda i: (i, 0)),
                  pl.BlockSpec((tg, ROUTER_LANES), lambda i: (i, 0)),
                  pl.BlockSpec(memory_space=pl.ANY)],
        out_specs=pl.BlockSpec((tg, D_MODEL), lambda i: (i, 0)),
        out_shape=jax.ShapeDtypeStruct((t, D_MODEL), F32),
        scratch_shapes=[pltpu.VMEM((tg, D_MODEL), F32), pltpu.VMEM((tg, D_MODEL), F32),
                        pltpu.SemaphoreType.DMA(()), pltpu.SemaphoreType.DMA(())],
        compiler_params=_cparams(("arbitrary",)),
        name="moe_combine",
    )(pos0.reshape(t // tg, 1, tg), pos1.reshape(t // tg, 1, tg), h, gates, ys)


def _moe(h, hn, idx, gates, w1, w3, w2):
    t = h.shape[0]
    tm = min(ROW_TILE, t)
    n_tiles = (t * TOP_K) // tm + N_EXPERTS
    n_rows = n_tiles * tm
    flat_e = idx[:, :TOP_K].reshape(-1)
    onehot = (flat_e[:, None] == jnp.arange(N_EXPERTS)[None, :]).astype(jnp.int32)
    rank = jnp.sum((jnp.cumsum(onehot, axis=0) - onehot) * onehot, axis=1)
    counts = jnp.sum(onehot, axis=0)
    padded = ((counts + tm - 1) // tm) * tm
    ends = jnp.cumsum(padded)
    starts = ends - padded
    pos = starts[flat_e] + rank
    token_of = jnp.zeros((n_rows,), jnp.int32).at[pos].set(jnp.arange(t * TOP_K, dtype=jnp.int32) // TOP_K)
    tile_start = jnp.arange(n_tiles, dtype=jnp.int32) * tm
    tile_valid = (tile_start < ends[-1]).astype(jnp.int32)
    tile_expert = jnp.minimum(jnp.sum((tile_start[:, None] >= ends[None, :]).astype(jnp.int32), axis=1),
                              N_EXPERTS - 1)
    last_expert = tile_expert[jnp.maximum(ends[-1] // tm - 1, 0)]
    tile_expert = jnp.where(tile_valid > 0, tile_expert, last_expert).astype(jnp.int32)
    xs = _gather_rows(hn, token_of)
    ys = _experts(xs, tile_expert, tile_valid, w1, w3, w2, tm)
    pos = pos.reshape(t, TOP_K).astype(jnp.int32)
    return _combine(h, gates, ys, pos[:, 0], pos[:, 1])


def _permute_in_cols(w):
    o = V_END
    wd = RWKV_WIDTH
    r = (o, o + wd)
    xw = (r[1], r[1] + RWKV_DECAY_LORA)
    k = (xw[1], xw[1] + wd)
    v = (k[1], k[1] + wd)
    xa = (v[1], v[1] + RWKV_A_LORA)
    xg = (xa[1], xa[1] + RWKV_GATE_LORA)
    order = [(0, o), r, k, v, xg, xw, xa, (RWKV_END, IN_COLS)]
    return jnp.concatenate([w[..., a:b] for a, b in order], axis=-1)


def kernel(x, norm1_g, w_in, q_norm_g, k_norm_g, attn_sinks, rel_bias, rwkv_mu, rwkv_w0, rwkv_w2, rwkv_a0, rwkv_a2, rwkv_g2, rwkv_k_k, rwkv_k_a, rwkv_r_k, rwkv_ln_w, rwkv_ln_b, ssm_lambda_re, ssm_lambda_im, ssm_b_re, ssm_b_im, ssm_c_re, ssm_c_im, ssm_d, ssm_log_dt, ssm_glu_w, ssm_glu_b, attn_out_g, ssm_out_g, w_out, norm2_g, ffn_w1, ffn_w3, ffn_w2, moe_router, moe_w1, moe_w3, moe_w2):
    batch, seq, _ = x.shape
    depth = w_in.shape[0]
    h = x.reshape(batch * seq, D_MODEL)
    for i in range(depth):
        w_in_i = _permute_in_cols(w_in[i]).astype(BF16)
        mu_i = _permute_in_cols(jnp.pad(rwkv_mu[i], (V_END, SSM_WIDTH)))[V_END:RWKV_END]
        q, kv, zr, zs = _in_proj(h, norm1_g[i], w_in_i)
        attn = _attention(q, kv, batch, seq, q_norm_g[i], k_norm_g[i], attn_sinks[i], rel_bias,
                          attn_out_g[i])
        rw = _rwkv(zr, batch, seq, mu_i, rwkv_w0[i], rwkv_w2[i], rwkv_a0[i], rwkv_a2[i], rwkv_g2[i],
                   rwkv_k_k[i], rwkv_k_a[i], rwkv_r_k[i], rwkv_ln_w[i], rwkv_ln_b[i])
        ss = _s5(zs, batch, seq, ssm_lambda_re[i], ssm_lambda_im[i], ssm_b_re[i], ssm_b_im[i],
                 ssm_c_re[i], ssm_c_im[i], ssm_d[i], ssm_log_dt[i], ssm_glu_w[i], ssm_glu_b[i],
                 ssm_out_g[i])
        j = i // 2
        if i % 2 == 0:
            h, hn = _out_proj(h, attn, rw, ss, w_out[i].astype(BF16), norm2_g[i])
            h = _ffn(h, hn, ffn_w1[j].astype(BF16), ffn_w3[j].astype(BF16), ffn_w2[j].astype(BF16))
        else:
            h, hn, idx, gates = _out_proj(h, attn, rw, ss, w_out[i].astype(BF16), norm2_g[i],
                                          router=moe_router[j])
            h = _moe(h, hn, idx, gates, moe_w1[j].astype(BF16), moe_w3[j].astype(BF16),
                     moe_w2[j].astype(BF16))
    return h.reshape(batch, seq, D_MODEL)
```

```python
import functools
import math

import numpy as np
import jax
import jax.numpy as jnp
from jax import lax
from jax.experimental import pallas as pl
from jax.experimental.pallas import tpu as pltpu

F32 = jnp.float32
BF16 = jnp.bfloat16

D_MODEL = 2048
HEAD_DIM = 64
ATTN_WIDTH = 1024
RWKV_WIDTH = 512
SSM_WIDTH = 512
ATTN_HEADS = 16
ATTN_KV_HEADS = 4
ATTN_GROUP = 4
KV_WIDTH = 256
WINDOW = 128
BLOCK = 128
REL_BUCKETS = 32
REL_MAX_DISTANCE = 128
RWKV_HEADS = 8
RWKV_DECAY_LORA = 64
RWKV_A_LORA = 64
RWKV_GATE_LORA = 128
RWKV_COLS = 1792
RWKV_DECAY_SCALE = math.exp(-0.5)
RWKV_LN_EPS = HEAD_DIM * 1e-5
SSM_GROUP_CH = 16
SSM_GROUPS = 32
SSM_STATE = 64
SSM_STATES = SSM_GROUPS * SSM_STATE
Q_END = ATTN_WIDTH
K_END = Q_END + KV_WIDTH
V_END = K_END + KV_WIDTH
RWKV_END = V_END + RWKV_COLS
IN_COLS = RWKV_END + SSM_WIDTH
D_FF = 5632
N_EXPERTS = 8
TOP_K = 2
D_FF_EXPERT = 7168
NORM_EPS = 1e-6

LANES = 128
MXU_DIM = 256
VMEM_LIMIT = 56 * 1024 * 1024

ROW_TILE = 512
FFN_COL_TILE = 512
SCAN_TILE = 256
S5_SEGMENTS = 8
S5_STRIP = 512
RWKV_CHUNK = 64
RWKV_QUAD = 4
GATHER_TILE = 512
GATHER_UNROLL = 8
ROUTER_LANES = 128


def _cparams(sem):
    return pltpu.CompilerParams(dimension_semantics=sem, vmem_limit_bytes=VMEM_LIMIT)


def _mm(a, b):
    return jnp.dot(a.astype(BF16), b.astype(BF16), preferred_element_type=F32)


def _mm_nt(a, b):
    return lax.dot_general(a.astype(BF16), b.astype(BF16), (((1,), (1,)), ((), ())),
                           preferred_element_type=F32)


def _mm_tn(a, b):
    return lax.dot_general(a.astype(BF16), b.astype(BF16), (((0,), (0,)), ((), ())),
                           preferred_element_type=F32)


def _split2(x):
    hi = x.astype(BF16)
    lo = (x - hi.astype(F32)).astype(BF16)
    return hi, lo


def _split3(x):
    h1 = x.astype(BF16)
    r1 = x - h1.astype(F32)
    h2 = r1.astype(BF16)
    h3 = (r1 - h2.astype(F32)).astype(BF16)
    return h1, h2, h3


def _seg_sum(x, ones_bd):
    hi, lo = _split2(x)
    return (jnp.dot(hi, ones_bd, preferred_element_type=F32)
            + jnp.dot(lo, ones_bd, preferred_element_type=F32))


def _const_spec(shape):
    nd = len(shape)
    return pl.BlockSpec(shape, lambda *_: (0,) * nd, pipeline_mode=pl.Buffered(1))


def _in_proj_kernel(x_ref, g_ref, w_ref, q_ref, kv_ref, rw_ref, ss_ref):
    x = x_ref[...]
    ms = jnp.mean(x * x, axis=-1, keepdims=True)
    hn = (x * lax.rsqrt(ms + NORM_EPS) * g_ref[...]).astype(BF16)
    q_ref[...] = jnp.dot(hn, w_ref[:, 0:Q_END], preferred_element_type=F32)
    kv_ref[...] = jnp.dot(hn, w_ref[:, Q_END:V_END], preferred_element_type=F32)
    rw_ref[...] = jnp.dot(hn, w_ref[:, V_END:RWKV_END], preferred_element_type=F32)
    ss_ref[...] = jnp.dot(hn, w_ref[:, RWKV_END:IN_COLS], preferred_element_type=F32)


def _in_proj(h, g, w):
    t = h.shape[0]
    tm = min(ROW_TILE, t)
    row = lambda w_: pl.BlockSpec((tm, w_), lambda i: (i, 0))
    return pl.pallas_call(
        _in_proj_kernel,
        grid=(t // tm,),
        in_specs=[row(D_MODEL), _const_spec((1, D_MODEL)), _const_spec((D_MODEL, IN_COLS))],
        out_specs=[row(ATTN_WIDTH), row(2 * KV_WIDTH), row(RWKV_COLS), row(SSM_WIDTH)],
        out_shape=[jax.ShapeDtypeStruct((t, ATTN_WIDTH), F32),
                   jax.ShapeDtypeStruct((t, 2 * KV_WIDTH), F32),
                   jax.ShapeDtypeStruct((t, RWKV_COLS), F32),
                   jax.ShapeDtypeStruct((t, SSM_WIDTH), F32)],
        compiler_params=_cparams(("arbitrary",)),
        name="in_proj",
    )(h, g.reshape(1, D_MODEL), w)


def _attn_kernel(sink_ref, q_ref, kvp_ref, kvc_ref, bias_ref, qg_ref, kg_ref, og_ref, e_ref, dup_ref,
                 o_ref, acc_ref):
    first = pl.program_id(1) == 0
    e = e_ref[...]
    kv = jnp.concatenate([kvp_ref[...], kvc_ref[...]], axis=0)
    rows = ATTN_GROUP * BLOCK

    def head_norm(t, g):
        ss = _seg_sum(t * t, e)
        return t * lax.rsqrt(ss * (1.0 / HEAD_DIM) + NORM_EPS) * g

    kn = head_norm(kv[:, :KV_WIDTH], kg_ref[...]).astype(BF16)
    v_dup = jnp.dot(kv[:, KV_WIDTH:].astype(BF16), dup_ref[...], preferred_element_type=F32).astype(BF16)
    ones = jnp.ones((2 * BLOCK, LANES), BF16)
    key_idx = lax.broadcasted_iota(jnp.int32, (rows, 2 * BLOCK), 1)
    no_prev = jnp.logical_and(first, key_idx < BLOCK)
    group = lax.broadcasted_iota(jnp.int32, (rows, 1), 0) // BLOCK
    low_half = lax.broadcasted_iota(jnp.int32, (1, LANES), 1) < HEAD_DIM
    scale = HEAD_DIM ** -0.5

    def scores(h):
        qn = (head_norm(q_ref[:, h * MXU_DIM:(h + 1) * MXU_DIM], qg_ref[...]) * scale).astype(BF16)
        qs = jnp.concatenate([qn[:, g * HEAD_DIM:(g + 1) * HEAD_DIM] for g in range(ATTN_GROUP)], axis=0)
        bias = bias_ref[h * ATTN_GROUP:(h + 1) * ATTN_GROUP].reshape(rows, 2 * BLOCK)
        return jnp.where(no_prev, -jnp.inf, _mm_nt(qs, kn[:, h * HEAD_DIM:(h + 1) * HEAD_DIM]) + bias)

    def finish(h, s):
        sink = jnp.full((rows, 1), sink_ref[h * ATTN_GROUP], F32)
        for g in range(1, ATTN_GROUP):
            sink = jnp.where(group == g, sink_ref[h * ATTN_GROUP + g], sink)
        m = jnp.maximum(jnp.max(s, axis=-1, keepdims=True), sink)
        p = jnp.exp(s - m).astype(BF16)
        rhs = jnp.concatenate([v_dup[:, h * LANES:(h + 1) * LANES], ones], axis=1)
        pv = jnp.dot(p, rhs, preferred_element_type=F32)
        out = pv[:, :LANES] / (pv[:, LANES:] + jnp.exp(sink - m))
        for pair in range(ATTN_GROUP // 2):
            even = out[(2 * pair) * BLOCK:(2 * pair + 1) * BLOCK]
            odd = out[(2 * pair + 1) * BLOCK:(2 * pair + 2) * BLOCK]
            col = (h * ATTN_GROUP + 2 * pair) * HEAD_DIM
            acc_ref[:, col:col + LANES] = jnp.where(low_half, even, odd)

    pending = scores(0)
    for h in range(ATTN_KV_HEADS):
        nxt = scores(h + 1) if h + 1 < ATTN_KV_HEADS else None
        finish(h, pending)
        pending = nxt
    o = acc_ref[...]
    ms = jnp.mean(o * o, axis=-1, keepdims=True)
    o_ref[...] = (o * lax.rsqrt(ms + NORM_EPS) * og_ref[...]).astype(BF16)


def _t5_bucket_table():
    qi = np.arange(BLOCK)[:, None]
    kj = np.arange(2 * BLOCK)[None, :]
    dist = BLOCK + qi - kj
    max_exact = REL_BUCKETS // 2
    d = np.maximum(dist, 0)
    large = max_exact + (np.log(np.maximum(d, 1).astype(np.float32) / max_exact)
                         / math.log(REL_MAX_DISTANCE / max_exact)
                         * (REL_BUCKETS - max_exact)).astype(np.int32)
    large = np.minimum(large, REL_BUCKETS - 1)
    bucket = np.where(d < max_exact, d, large)
    valid = (dist >= 0) & (dist < WINDOW)
    return bucket, valid


def _attention(q, kv, batch, seq, q_g, k_g, sinks, rel_bias, out_g):
    nb = seq // BLOCK
    bucket, valid = _t5_bucket_table()
    bias = jnp.transpose(rel_bias.astype(F32)[bucket], (2, 0, 1))
    bias = jnp.where(valid[None], bias, -jnp.inf)
    seg = np.arange(MXU_DIM) // HEAD_DIM
    ones_bd = jnp.asarray(seg[:, None] == seg[None, :], dtype=BF16)
    dup_src = (np.arange(2 * KV_WIDTH) // LANES) * HEAD_DIM + np.arange(2 * KV_WIDTH) % HEAD_DIM
    dup = jnp.asarray(np.arange(KV_WIDTH)[:, None] == dup_src[None, :], dtype=BF16)
    tile4 = lambda g: jnp.tile(g.astype(F32), ATTN_GROUP).reshape(1, MXU_DIM)
    grid_spec = pltpu.PrefetchScalarGridSpec(
        num_scalar_prefetch=1,
        grid=(batch, nb),
        in_specs=[
            pl.BlockSpec((BLOCK, ATTN_WIDTH), lambda b, n, s: (b * nb + n, 0)),
            pl.BlockSpec((BLOCK, 2 * KV_WIDTH), lambda b, n, s: (jnp.maximum(b * nb + n - 1, 0), 0)),
            pl.BlockSpec((BLOCK, 2 * KV_WIDTH), lambda b, n, s: (b * nb + n, 0)),
            pl.BlockSpec((ATTN_HEADS, BLOCK, 2 * BLOCK), lambda b, n, s: (0, 0, 0)),
            pl.BlockSpec((1, MXU_DIM), lambda b, n, s: (0, 0)),
            pl.BlockSpec((1, MXU_DIM), lambda b, n, s: (0, 0)),
            pl.BlockSpec((1, ATTN_WIDTH), lambda b, n, s: (0, 0)),
            pl.BlockSpec((MXU_DIM, MXU_DIM), lambda b, n, s: (0, 0)),
            pl.BlockSpec((KV_WIDTH, 2 * KV_WIDTH), lambda b, n, s: (0, 0)),
        ],
        out_specs=pl.BlockSpec((BLOCK, ATTN_WIDTH), lambda b, n, s: (b * nb + n, 0)),
        scratch_shapes=[pltpu.VMEM((BLOCK, ATTN_WIDTH), F32)],
    )
    return pl.pallas_call(
        _attn_kernel,
        grid_spec=grid_spec,
        out_shape=jax.ShapeDtypeStruct((batch * seq, ATTN_WIDTH), BF16),
        compiler_params=_cparams(("arbitrary", "arbitrary")),
        name="swa",
    )(sinks.astype(F32), q, kv, kv, bias, tile4(q_g), tile4(k_g), out_g.reshape(1, ATTN_WIDTH), ones_bd,
      dup)


def _rwkv_consts():
    c, g_heads = RWKV_CHUNK, RWKV_QUAD
    w = g_heads * HEAD_DIM
    row = lax.broadcasted_iota(jnp.int32, (c, g_heads * c), 0)
    col = lax.broadcasted_iota(jnp.int32, (c, g_heads * c), 1) % c
    strict = col < row
    eye = (row == col).astype(F32)
    diag2 = jnp.logical_and((row // 2) == (col // 2), strict)
    levels = []
    bs = 2
    while bs < c:
        levels.append(((row // (2 * bs)) == (col // (2 * bs))) & ((row // bs) % 2 == 1)
                      & ((col // bs) % 2 == 0))
        bs *= 2
    hrow = lax.broadcasted_iota(jnp.int32, (w, w), 0) // HEAD_DIM
    hcol = lax.broadcasted_iota(jnp.int32, (w, w), 1) // HEAD_DIM
    return strict, col <= row, eye, diag2, levels, hrow == hcol


def _rwkv_stack(x, bm):
    return jnp.tile(x.astype(BF16), (RWKV_QUAD, 1)) * bm


def _rwkv_prepare(items, bm, consts):
    strict, incl, eye, diag2, levels, _ = consts
    c = RWKV_CHUNK
    gc = RWKV_QUAD * c
    nt = (((1,), (1,)), ((), ()))
    out = []
    for (r, lw, cum, k, v, a, b) in items:
        last = cum[c - 1:c]
        p_inv = jnp.exp(-cum)
        ratio = jnp.exp(last - cum)
        out.append(dict(
            ar=jnp.concatenate([a * jnp.exp(cum - lw), r * jnp.exp(cum)], axis=0).astype(BF16),
            bks=jnp.concatenate([_rwkv_stack(b * p_inv, bm), _rwkv_stack(k * p_inv, bm)], axis=0),
            v=v, v_stack=_rwkv_stack(v, bm),
            uv_rhs=jnp.concatenate([b * ratio, k * ratio], axis=0).astype(BF16),
            p_last=jnp.exp(last)))
    bigs = [lax.dot_general(d["ar"], d.pop("bks"), nt, preferred_element_type=F32) for d in out]
    n_cats, t_invs = [], []
    for d, big in zip(out, bigs):
        n_cat = jnp.where(strict, big[:c, :gc], 0.0)
        n_cats.append(n_cat)
        t_invs.append(eye + jnp.where(diag2, n_cat, 0.0))
        d["r_b"] = jnp.where(incl, big[c:, :gc], 0.0).astype(BF16)
        d["akrk"] = jnp.concatenate([jnp.where(strict, big[:c, gc:], 0.0),
                                     jnp.where(incl, big[c:, gc:], 0.0)], axis=0).astype(BF16)
    for lm in levels:
        tns = [_mm(t, _rwkv_stack(jnp.where(lm, n, 0.0), bm)) for t, n in zip(t_invs, n_cats)]
        t_invs = [t + _mm(tn, _rwkv_stack(t, bm)) for t, tn in zip(t_invs, tns)]
    for d, t in zip(out, t_invs):
        d["t_inv"] = t.astype(BF16)
        d["wy"] = jnp.dot(d.pop("akrk"), d.pop("v_stack"), preferred_element_type=F32)
    return out


def _rwkv_apply(ds, hts, bm, consts):
    c = RWKV_CHUNK
    nt = (((1,), (1,)), ((), ()))
    ahs = [lax.dot_general(d["ar"], ht.astype(BF16), nt, preferred_element_type=F32)
           for d, ht in zip(ds, hts)]
    us = [_mm(d["t_inv"], _rwkv_stack(ah[:c] + d["wy"][:c], bm)) for d, ah in zip(ds, ahs)]
    upds = [_mm_tn(jnp.concatenate([u, d["v"]], axis=0), d["uv_rhs"]) for d, u in zip(ds, us)]
    ys = [ah[c:] + _mm(d["r_b"], _rwkv_stack(u, bm)) + d["wy"][c:] for d, ah, u in zip(ds, ahs, us)]
    new = [ht * d["p_last"] + jnp.where(consts[5], upd, 0.0) for d, ht, upd in zip(ds, hts, upds)]
    return ys, new


def _rwkv_kernel(z_ref, mu_ref, w2a_ref, g2_ref, w0_ref, a0_ref, kk_ref, ka_ref, rk_ref, lnw_ref,
                 lnb_ref, e_ref, tri_ref, bm_ref, o_ref, carry_ref, y_s, h_s):
    tb = z_ref.shape[0]
    wd = RWKV_WIDTH
    quad_w = RWKV_QUAD * HEAD_DIM
    n_quads = RWKV_HEADS // RWKV_QUAD
    c = RWKV_CHUNK

    @pl.when(pl.program_id(1) == 0)
    def _():
        carry_ref[...] = jnp.zeros_like(carry_ref)
        h_s[...] = jnp.zeros_like(h_s)

    z = z_ref[...]
    rows = lax.broadcasted_iota(jnp.int32, (tb, 1), 0)
    prev = jnp.where(rows == 0, carry_ref[...], pltpu.roll(z, 1, axis=0))
    carry_ref[...] = z[tb - 1:tb]
    zs = z + mu_ref[...] * (prev - z)
    r = zs[:, 0:wd]
    k = zs[:, wd:2 * wd]
    v = zs[:, 2 * wd:3 * wd]
    xg = zs[:, 3 * wd:3 * wd + RWKV_GATE_LORA]
    xwa = zs[:, 3 * wd + RWKV_GATE_LORA:]
    lane = lax.broadcasted_iota(jnp.int32, (1, RWKV_DECAY_LORA + RWKV_A_LORA), 1)
    lora_in = jnp.where(lane < RWKV_DECAY_LORA, jnp.tanh(xwa), xwa)
    lora = _mm(lora_in, w2a_ref[...])
    lw = -RWKV_DECAY_SCALE * jax.nn.sigmoid(w0_ref[...] + lora[:, :wd])
    iclr = jax.nn.sigmoid(a0_ref[...] + lora[:, wd:])
    gate = _mm(jax.nn.sigmoid(xg), g2_ref[...])
    kk = k * kk_ref[...]
    e = e_ref[...]
    seg = lambda x: jnp.concatenate(
        [_seg_sum(x[:, q * quad_w:(q + 1) * quad_w], e) for q in range(n_quads)], axis=1)
    kk = kk * lax.rsqrt(jnp.maximum(seg(kk * kk), 1e-12))
    k2 = k * (1.0 + (iclr - 1.0) * ka_ref[...])
    l1, l2, l3 = _split3(lw)
    tri = tri_ref[...]
    cum = (jnp.dot(tri, l1, preferred_element_type=F32) + jnp.dot(tri, l2, preferred_element_type=F32)
           + jnp.dot(tri, l3, preferred_element_type=F32))
    a = -kk
    b = kk * iclr

    consts = _rwkv_consts()
    bm = bm_ref[...]
    n_chunks = tb // c
    items = []
    for ci in range(n_chunks):
        for q in range(n_quads):
            sl = (slice(ci * c, (ci + 1) * c), slice(q * quad_w, (q + 1) * quad_w))
            items.append((r[sl], lw[sl], cum[sl], k2[sl], v[sl], a[sl], b[sl]))
    prepared = _rwkv_prepare(items, bm, consts)
    hts = [h_s[q] for q in range(n_quads)]
    for ci in range(n_chunks):
        ys, hts = _rwkv_apply(prepared[ci * n_quads:(ci + 1) * n_quads], hts, bm, consts)
        for q in range(n_quads):
            y_s[ci * c:(ci + 1) * c, q * quad_w:(q + 1) * quad_w] = ys[q]
    for q in range(n_quads):
        h_s[q] = hts[q]

    y = y_s[...]
    mean = seg(y) * (1.0 / HEAD_DIM)
    d = y - mean
    var = seg(d * d) * (1.0 / HEAD_DIM)
    yn = d * lax.rsqrt(var + RWKV_LN_EPS) * lnw_ref[...] + lnb_ref[...]
    bonus = seg(r * k2 * rk_ref[...]) * v
    o_ref[...] = ((yn + bonus) * gate).astype(BF16)


def _rwkv(z, batch, seq, mu, w0, w2, a0, a2, g2, k_k, k_a, r_k, ln_w, ln_b):
    tb = min(SCAN_TILE, seq)
    nb = seq // tb
    wd = RWKV_WIDTH
    quad_w = RWKV_QUAD * HEAD_DIM
    w2a = jnp.zeros((RWKV_DECAY_LORA + RWKV_A_LORA, 2 * wd), F32)
    w2a = w2a.at[:RWKV_DECAY_LORA, :wd].set(w2).at[RWKV_DECAY_LORA:, wd:].set(a2).astype(BF16)
    seg = np.arange(quad_w) // HEAD_DIM
    ones_bd = jnp.asarray(seg[:, None] == seg[None, :], dtype=BF16)
    t_idx = np.arange(tb)
    tri = jnp.asarray((t_idx[:, None] // RWKV_CHUNK == t_idx[None, :] // RWKV_CHUNK)
                      & (t_idx[None, :] <= t_idx[:, None]), dtype=BF16)
    stack_rows = np.arange(RWKV_QUAD * RWKV_CHUNK) // RWKV_CHUNK
    stack_mask = jnp.asarray(stack_rows[:, None] == seg[None, :], dtype=BF16)
    vec = lambda p: p.astype(F32).reshape(1, -1)
    return pl.pallas_call(
        _rwkv_kernel,
        grid=(batch, nb),
        in_specs=[pl.BlockSpec((tb, RWKV_COLS), lambda b, j: (b * nb + j, 0)),
                  _const_spec((1, RWKV_COLS)),
                  _const_spec((RWKV_DECAY_LORA + RWKV_A_LORA, 2 * wd)),
                  _const_spec((RWKV_GATE_LORA, wd))]
                 + [_const_spec((1, wd))] * 7
                 + [_const_spec((quad_w, quad_w)), _const_spec((tb, tb)),
                    _const_spec((RWKV_QUAD * RWKV_CHUNK, quad_w))],
        out_specs=pl.BlockSpec((tb, wd), lambda b, j: (b * nb + j, 0)),
        out_shape=jax.ShapeDtypeStruct((batch * seq, wd), BF16),
        scratch_shapes=[pltpu.VMEM((1, RWKV_COLS), F32), pltpu.VMEM((tb, wd), F32),
                        pltpu.VMEM((RWKV_HEADS // RWKV_QUAD, quad_w, quad_w), F32)],
        compiler_params=_cparams(("arbitrary", "arbitrary")),
        name="rwkv7",
    )(z, vec(mu), w2a, g2.astype(BF16), vec(w0), vec(a0), vec(k_k), vec(k_a), vec(r_k), vec(ln_w),
      vec(ln_b), ones_bd, tri, stack_mask)


def _cmul_add(ar, ai, xr, xi, br, bi):
    return ar * xr - ai * xi + br, ar * xi + ai * xr + bi


def _s5_kernel(u_ref, perm_ref, bbr_ref, bbi_ref, cr_ref, ci_ref, a1_ref, aseg_ref, apj_ref, ask_ref,
               d_ref, gw_ref, gb_ref, og_ref, o_ref, xr_s, xi_s, car_s, cai_s):
    tb = u_ref.shape[0]
    seg_len = tb // S5_SEGMENTS
    strip = S5_STRIP
    n_strips = SSM_STATES // strip
    n_groups = SSM_WIDTH // LANES

    @pl.when(pl.program_id(1) == 0)
    def _():
        car_s[...] = jnp.zeros_like(car_s)
        cai_s[...] = jnp.zeros_like(cai_s)

    perm = perm_ref[...]
    u1, u2, u3 = _split3(u_ref[...])
    u = (jnp.dot(perm, u1, preferred_element_type=F32) + jnp.dot(perm, u2, preferred_element_type=F32)
         + jnp.dot(perm, u3, preferred_element_type=F32))
    ub = u.astype(BF16)
    sw = SSM_STATES // n_groups
    for g in range(n_groups):
        ug = ub[:, g * LANES:(g + 1) * LANES]
        xr_s[:, g * sw:(g + 1) * sw] = jnp.dot(ug, bbr_ref[g * LANES:(g + 1) * LANES, g * sw:(g + 1) * sw],
                                               preferred_element_type=F32)
        xi_s[:, g * sw:(g + 1) * sw] = jnp.dot(ug, bbi_ref[g * LANES:(g + 1) * LANES, g * sw:(g + 1) * sw],
                                               preferred_element_type=F32)

    for s in range(n_strips):
        cols = slice(s * strip, (s + 1) * strip)
        ar = jnp.broadcast_to(a1_ref[0:1, cols], (S5_SEGMENTS, strip))
        ai = jnp.broadcast_to(a1_ref[1:2, cols], (S5_SEGMENTS, strip))

        def scan_body(j, x):
            rs = pl.ds(pl.multiple_of(j * S5_SEGMENTS, S5_SEGMENTS), S5_SEGMENTS)
            nr, ni = _cmul_add(ar, ai, x[0], x[1], xr_s[rs, cols], xi_s[rs, cols])
            xr_s[rs, cols] = nr
            xi_s[rs, cols] = ni
            return nr, ni

        fr, fi = lax.fori_loop(1, seg_len, scan_body,
                               (xr_s[0:S5_SEGMENTS, cols], xi_s[0:S5_SEGMENTS, cols]), unroll=2)
        seg_row = lax.broadcasted_iota(jnp.int32, (S5_SEGMENTS, strip), 0)
        ir, ii = fr, fi
        for lvl in range(int(math.log2(S5_SEGMENTS))):
            sh = 1 << lvl
            pr = jnp.where(seg_row >= sh, pltpu.roll(ir, sh, axis=0), 0.0)
            pi = jnp.where(seg_row >= sh, pltpu.roll(ii, sh, axis=0), 0.0)
            ir, ii = _cmul_add(aseg_ref[2 * lvl:2 * lvl + 1, cols], aseg_ref[2 * lvl + 1:2 * lvl + 2, cols],
                               pr, pi, ir, ii)
        c_in_r = car_s[:, cols]
        c_in_i = cai_s[:, cols]
        er = jnp.where(seg_row >= 1, pltpu.roll(ir, 1, axis=0), 0.0)
        ei = jnp.where(seg_row >= 1, pltpu.roll(ii, 1, axis=0), 0.0)
        cr_, ci_ = _cmul_add(ask_ref[0:S5_SEGMENTS, cols], ask_ref[S5_SEGMENTS:2 * S5_SEGMENTS, cols],
                             c_in_r, c_in_i, er, ei)
        last = S5_SEGMENTS - 1
        nr, ni = _cmul_add(aseg_ref[0:1, cols], aseg_ref[1:2, cols],
                           cr_[last:last + 1], ci_[last:last + 1], fr[last:last + 1], fi[last:last + 1])
        car_s[:, cols] = nr
        cai_s[:, cols] = ni

        def fix_body(j, carry):
            rs = pl.ds(pl.multiple_of(j * S5_SEGMENTS, S5_SEGMENTS), S5_SEGMENTS)
            pr = apj_ref[pl.ds(2 * j, 1), cols]
            pi = apj_ref[pl.ds(2 * j + 1, 1), cols]
            nr, ni = _cmul_add(pr, pi, cr_, ci_, xr_s[rs, cols], xi_s[rs, cols])
            xr_s[rs, cols] = nr
            xi_s[rs, cols] = ni
            return carry

        lax.fori_loop(0, seg_len, fix_body, 0, unroll=2)

    sg = SSM_STATES // n_groups
    ys = []
    for g in range(n_groups):
        xr = xr_s[:, g * sg:(g + 1) * sg].astype(BF16)
        xi = xi_s[:, g * sg:(g + 1) * sg].astype(BF16)
        ys.append(jnp.dot(xr, cr_ref[g * sg:(g + 1) * sg, g * LANES:(g + 1) * LANES], preferred_element_type=F32)
                  - jnp.dot(xi, ci_ref[g * sg:(g + 1) * sg, g * LANES:(g + 1) * LANES], preferred_element_type=F32))
    y = jnp.concatenate(ys, axis=1) + d_ref[...] * u
    zg = jax.nn.gelu(y)
    gl = _mm(zg, gw_ref[...]) + gb_ref[...]
    out = zg * jax.nn.sigmoid(gl)
    ms = jnp.mean(out * out, axis=-1, keepdims=True)
    out = (out * lax.rsqrt(ms + NORM_EPS) * og_ref[...]).astype(BF16)
    o_ref[...] = lax.dot_general(perm, out, (((0,), (0,)), ((), ())),
                                 preferred_element_type=F32).astype(BF16)


def _s5(u, batch, seq, lam_re, lam_im, b_re, b_im, c_re, c_im, d, log_dt, glu_w, glu_b, out_g):
    tb = min(SCAN_TILE, seq)
    nb = seq // tb
    seg_len = tb // S5_SEGMENTS
    f = lambda p: p.astype(F32)
    lr, li = f(lam_re), f(lam_im)
    dt = jnp.exp(f(log_dt))[:, None]
    mag = jnp.exp(lr * dt)
    ab_re = mag * jnp.cos(li * dt)
    ab_im = mag * jnp.sin(li * dt)
    den = lr * lr + li * li
    fr = ((ab_re - 1.0) * lr + ab_im * li) / den
    fi = (ab_im * lr - (ab_re - 1.0) * li) / den
    bb_re = fr[..., None] * f(b_re) - fi[..., None] * f(b_im)
    bb_im = fr[..., None] * f(b_im) + fi[..., None] * f(b_re)
    eye = jnp.eye(SSM_GROUPS, dtype=F32)
    in_bd = lambda t: jnp.einsum('gpm,gh->gmhp', t, eye).reshape(SSM_WIDTH, SSM_STATES).astype(BF16)
    out_bd = lambda t: jnp.einsum('gmp,gh->gphm', f(t), eye).reshape(SSM_STATES, SSM_WIDTH).astype(BF16)

    def a_pow(n):
        n = jnp.asarray(n, F32)[:, None, None]
        m = jnp.exp(n * (lr * dt))
        ang = n * (li * dt)
        both = jnp.stack([m * jnp.cos(ang), m * jnp.sin(ang)], axis=1)
        return both.reshape(-1, SSM_STATES)

    a1 = a_pow([1])
    aseg = a_pow([seg_len * (1 << l) for l in range(4)])
    apj = a_pow(np.arange(1, seg_len + 1))
    ask_both = a_pow(np.arange(S5_SEGMENTS) * seg_len).reshape(S5_SEGMENTS, 2, SSM_STATES)
    ask = jnp.concatenate([ask_both[:, 0], ask_both[:, 1]], axis=0)
    t_idx = np.arange(tb)
    src_time = (t_idx % S5_SEGMENTS) * seg_len + t_idx // S5_SEGMENTS
    perm = jnp.asarray(src_time[:, None] == t_idx[None, :], dtype=BF16)
    vec = lambda p: f(p).reshape(1, -1)
    buf = lambda: pltpu.VMEM((tb, SSM_STATES), F32)
    return pl.pallas_call(
        _s5_kernel,
        grid=(batch, nb),
        in_specs=[pl.BlockSpec((tb, SSM_WIDTH), lambda b, j: (b * nb + j, 0)),
                  _const_spec((tb, tb)),
                  _const_spec((SSM_WIDTH, SSM_STATES)), _const_spec((SSM_WIDTH, SSM_STATES)),
                  _const_spec((SSM_STATES, SSM_WIDTH)), _const_spec((SSM_STATES, SSM_WIDTH)),
                  _const_spec((2, SSM_STATES)), _const_spec((8, SSM_STATES)),
                  _const_spec((2 * seg_len, SSM_STATES)), _const_spec((2 * S5_SEGMENTS, SSM_STATES)),
                  _const_spec((1, SSM_WIDTH)), _const_spec((SSM_WIDTH, SSM_WIDTH)),
                  _const_spec((1, SSM_WIDTH)), _const_spec((1, SSM_WIDTH))],
        out_specs=pl.BlockSpec((tb, SSM_WIDTH), lambda b, j: (b * nb + j, 0)),
        out_shape=jax.ShapeDtypeStruct((batch * seq, SSM_WIDTH), BF16),
        scratch_shapes=[buf(), buf(),
                        pltpu.VMEM((1, SSM_STATES), F32), pltpu.VMEM((1, SSM_STATES), F32)],
        compiler_params=_cparams(("arbitrary", "arbitrary")),
        name="s5",
    )(u, perm, in_bd(bb_re), in_bd(bb_im), out_bd(c_re), out_bd(c_im), a1, aseg, apj, ask, vec(d),
      glu_w.astype(BF16), vec(glu_b), vec(out_g))


def _out_proj_kernel(h_ref, at_ref, rw_ref, ss_ref, w_ref, g_ref, *rest, with_router):
    if with_router:
        rt_ref, ho_ref, hn_ref, idx_ref, gate_ref = rest
    else:
        ho_ref, hn_ref = rest
    mix = (jnp.dot(at_ref[...], w_ref[0:ATTN_WIDTH, :], preferred_element_type=F32)
           + jnp.dot(rw_ref[...], w_ref[ATTN_WIDTH:ATTN_WIDTH + RWKV_WIDTH, :], preferred_element_type=F32)
           + jnp.dot(ss_ref[...], w_ref[ATTN_WIDTH + RWKV_WIDTH:, :], preferred_element_type=F32))
    h = h_ref[...] + mix
    ho_ref[...] = h
    ms = jnp.mean(h * h, axis=-1, keepdims=True)
    hn = h * lax.rsqrt(ms + NORM_EPS) * g_ref[...]
    if not with_router:
        hn_ref[...] = hn.astype(BF16)
        return
    hn_ref[...] = hn
    h_hi, h_lo = _split2(hn)
    r_hi, r_lo = _split2(rt_ref[...])
    logits = (jnp.dot(h_hi, r_hi, preferred_element_type=F32)
              + jnp.dot(h_lo, r_hi, preferred_element_type=F32)
              + jnp.dot(h_hi, r_lo, preferred_element_type=F32))
    lane_i = lax.broadcasted_iota(jnp.int32, logits.shape, 1)
    lane = lane_i.astype(F32)
    logits = jnp.where(lane_i < N_EXPERTS, logits, -jnp.inf)
    m1 = jnp.max(logits, axis=-1, keepdims=True)
    i1 = jnp.min(jnp.where(logits == m1, lane, float(ROUTER_LANES)), axis=-1, keepdims=True)
    rest_l = jnp.where(lane == i1, -jnp.inf, logits)
    m2 = jnp.max(rest_l, axis=-1, keepdims=True)
    i2 = jnp.min(jnp.where(rest_l == m2, lane, float(ROUTER_LANES)), axis=-1, keepdims=True)
    e2 = jnp.exp(m2 - m1)
    g1 = 1.0 / (1.0 + e2)
    g2 = e2 / (1.0 + e2)
    idx_ref[...] = jnp.where(lane_i == 0, i1, jnp.where(lane_i == 1, i2, 0.0)).astype(jnp.int32)
    gate_ref[...] = jnp.where(lane_i == 0, g1, jnp.where(lane_i == 1, g2, 0.0))


def _out_proj(h, attn, rw, ss, w_out, g2, router=None):
    t = h.shape[0]
    tm = min(ROW_TILE, t)
    row = lambda w_: pl.BlockSpec((tm, w_), lambda i: (i, 0))
    with_router = router is not None
    in_specs = [row(D_MODEL), row(ATTN_WIDTH), row(RWKV_WIDTH), row(SSM_WIDTH),
                _const_spec((D_MODEL, D_MODEL)), _const_spec((1, D_MODEL))]
    args = [h, attn, rw, ss, w_out, g2.reshape(1, D_MODEL)]
    out_specs = [row(D_MODEL), row(D_MODEL)]
    out_shape = [jax.ShapeDtypeStruct((t, D_MODEL), F32),
                 jax.ShapeDtypeStruct((t, D_MODEL), F32 if with_router else BF16)]
    if with_router:
        rt = jnp.zeros((D_MODEL, ROUTER_LANES), F32).at[:, :N_EXPERTS].set(router.astype(F32))
        in_specs.append(_const_spec((D_MODEL, ROUTER_LANES)))
        args.append(rt)
        out_specs += [row(ROUTER_LANES), row(ROUTER_LANES)]
        out_shape += [jax.ShapeDtypeStruct((t, ROUTER_LANES), jnp.int32),
                      jax.ShapeDtypeStruct((t, ROUTER_LANES), F32)]
    return pl.pallas_call(
        functools.partial(_out_proj_kernel, with_router=with_router),
        grid=(t // tm,),
        in_specs=in_specs, out_specs=out_specs, out_shape=out_shape,
        compiler_params=_cparams(("arbitrary",)),
        name="out_proj_router" if with_router else "out_proj",
    )(*args)


def _ffn_kernel(h_ref, hn_ref, w1_ref, w3_ref, w2_ref, o_ref):
    f = pl.program_id(1)

    @pl.when(f == 0)
    def _():
        o_ref[...] = h_ref[...]

    hn = hn_ref[...]
    a1 = jnp.dot(hn, w1_ref[...], preferred_element_type=F32)
    a3 = jnp.dot(hn, w3_ref[...], preferred_element_type=F32)
    act = (a1 * jax.nn.sigmoid(a1) * a3).astype(BF16)
    o_ref[...] += jnp.dot(act, w2_ref[...], preferred_element_type=F32)


def _ffn(h, hn, w1, w3, w2):
    t = h.shape[0]
    tm = min(ROW_TILE, t)
    d_ff = w1.shape[1]
    tf = FFN_COL_TILE
    return pl.pallas_call(
        _ffn_kernel,
        grid=(t // tm, d_ff // tf),
        in_specs=[pl.BlockSpec((tm, D_MODEL), lambda i, f: (i, 0)),
                  pl.BlockSpec((tm, D_MODEL), lambda i, f: (i, 0)),
                  pl.BlockSpec((D_MODEL, tf), lambda i, f: (0, f)),
                  pl.BlockSpec((D_MODEL, tf), lambda i, f: (0, f)),
                  pl.BlockSpec((tf, D_MODEL), lambda i, f: (f, 0))],
        out_specs=pl.BlockSpec((tm, D_MODEL), lambda i, f: (i, 0)),
        out_shape=jax.ShapeDtypeStruct((t, D_MODEL), F32),
        compiler_params=_cparams(("arbitrary", "arbitrary")),
        name="ffn",
    )(h, hn, w1, w3, w2)


def _gather_rows_kernel(idx_ref, src_ref, o_ref, sem):
    rows = o_ref.shape[0]

    def copy(r):
        return pltpu.make_async_copy(src_ref.at[pl.ds(idx_ref[0, r], 1)], o_ref.at[pl.ds(r, 1)], sem)

    def issue(r, c):
        copy(r).start()
        return c

    def drain(r, c):
        copy(r).wait()
        return c

    lax.fori_loop(0, rows, issue, 0, unroll=GATHER_UNROLL)
    lax.fori_loop(0, rows, drain, 0, unroll=GATHER_UNROLL)


def _gather_rows(src, idx):
    n = idx.shape[0]
    tg = min(GATHER_TILE, n)
    width = src.shape[1]
    return pl.pallas_call(
        _gather_rows_kernel,
        grid=(n // tg,),
        in_specs=[pl.BlockSpec((None, 1, tg), lambda i: (i, 0, 0), memory_space=pltpu.SMEM),
                  pl.BlockSpec(memory_space=pl.ANY)],
        out_specs=pl.BlockSpec((tg, width), lambda i: (i, 0)),
        out_shape=jax.ShapeDtypeStruct((n, width), src.dtype),
        scratch_shapes=[pltpu.SemaphoreType.DMA(())],
        compiler_params=_cparams(("arbitrary",)),
        name="moe_gather",
    )(idx.reshape(n // tg, 1, tg), src)


def _expert_kernel(te_ref, tv_ref, x_ref, w1_ref, w3_ref, w2_ref, o_ref, xb_ref):
    i = pl.program_id(0)
    f = pl.program_id(1)

    @pl.when(f == 0)
    def _():
        xb_ref[...] = x_ref[...].astype(BF16)
        o_ref[...] = jnp.zeros_like(o_ref)

    @pl.when(tv_ref[i] > 0)
    def _():
        xb = xb_ref[...]
        a1 = jnp.dot(xb, w1_ref[...], preferred_element_type=F32)
        a3 = jnp.dot(xb, w3_ref[...], preferred_element_type=F32)
        act = (a1 * jax.nn.sigmoid(a1) * a3).astype(BF16)
        o_ref[...] += jnp.dot(act, w2_ref[...], preferred_element_type=F32)


def _experts(xs, tile_expert, tile_valid, w1, w3, w2, tm):
    n = xs.shape[0]
    d_ff = w1.shape[2]
    tf = FFN_COL_TILE
    nf = d_ff // tf
    fidx = lambda i, f, tv: jnp.where(tv[i] > 0, f, nf - 1)
    grid_spec = pltpu.PrefetchScalarGridSpec(
        num_scalar_prefetch=2,
        grid=(n // tm, nf),
        in_specs=[pl.BlockSpec((tm, D_MODEL), lambda i, f, te, tv: (i, 0)),
                  pl.BlockSpec((None, D_MODEL, tf), lambda i, f, te, tv: (te[i], 0, fidx(i, f, tv))),
                  pl.BlockSpec((None, D_MODEL, tf), lambda i, f, te, tv: (te[i], 0, fidx(i, f, tv))),
                  pl.BlockSpec((None, tf, D_MODEL), lambda i, f, te, tv: (te[i], fidx(i, f, tv), 0))],
        out_specs=pl.BlockSpec((tm, D_MODEL), lambda i, f, te, tv: (i, 0)),
        scratch_shapes=[pltpu.VMEM((tm, D_MODEL), BF16)],
    )
    return pl.pallas_call(
        _expert_kernel,
        grid_spec=grid_spec,
        out_shape=jax.ShapeDtypeStruct((n, D_MODEL), F32),
        compiler_params=_cparams(("arbitrary", "arbitrary")),
        name="moe_experts",
    )(tile_expert, tile_valid, xs, w1, w3, w2)


def _combine_kernel(p0_ref, p1_ref, h_ref, gate_ref, ys_ref, o_ref, b0, b1, sem0, sem1):
    rows = o_ref.shape[0]

    def copies(r):
        return (pltpu.make_async_copy(ys_ref.at[pl.ds(p0_ref[0, r], 1)], b0.at[pl.ds(r, 1)], sem0),
                pltpu.make_async_copy(ys_ref.at[pl.ds(p1_ref[0, r], 1)], b1.at[pl.ds(r, 1)], sem1))

    def issue(r, c):
        c0, c1 = copies(r)
        c0.start()
        c1.start()
        return c

    def drain(r, c):
        c0, c1 = copies(r)
        c0.wait()
        c1.wait()
        return c

    lax.fori_loop(0, rows, issue, 0, unroll=GATHER_UNROLL)
    lax.fori_loop(0, rows, drain, 0, unroll=GATHER_UNROLL)
    gate = gate_ref[...]
    o_ref[...] = h_ref[...] + gate[:, 0:1] * b0[...] + gate[:, 1:2] * b1[...]


def _combine(h, gates, ys, pos0, pos1):
    t = h.shape[0]
    tg = min(GATHER_TILE, t)
    smem = pl.BlockSpec((None, 1, tg), lambda i: (i, 0, 0), memory_space=pltpu.SMEM)
    return pl.pallas_call(
        _combine_kernel,
        grid=(t // tg,),
        in_specs=[smem, smem,
                  pl.BlockSpec((tg, D_MODEL), lambda i: (i, 0)),
                  pl.BlockSpec((tg, ROUTER_LANES), lambda i: (i, 0)),
                  pl.BlockSpec(memory_space=pl.ANY)],
        out_specs=pl.BlockSpec((tg, D_MODEL), lambda i: (i, 0)),
        out_shape=jax.ShapeDtypeStruct((t, D_MODEL), F32),
        scratch_shapes=[pltpu.VMEM((tg, D_MODEL), F32), pltpu.VMEM((tg, D_MODEL), F32),
                        pltpu.SemaphoreType.DMA(()), pltpu.SemaphoreType.DMA(())],
        compiler_params=_cparams(("arbitrary",)),
        name="moe_combine",
    )(pos0.reshape(t // tg, 1, tg), pos1.reshape(t // tg, 1, tg), h, gates, ys)


def _moe(h, hn, idx, gates, w1, w3, w2):
    t = h.shape[0]
    tm = min(ROW_TILE, t)
    n_tiles = (t * TOP_K) // tm + N_EXPERTS
    n_rows = n_tiles * tm
    flat_e = idx[:, :TOP_K].reshape(-1)
    onehot = (flat_e[:, None] == jnp.arange(N_EXPERTS)[None, :]).astype(jnp.int32)
    rank = jnp.sum((jnp.cumsum(onehot, axis=0) - onehot) * onehot, axis=1)
    counts = jnp.sum(onehot, axis=0)
    padded = ((counts + tm - 1) // tm) * tm
    ends = jnp.cumsum(padded)
    starts = ends - padded
    pos = starts[flat_e] + rank
    token_of = jnp.zeros((n_rows,), jnp.int32).at[pos].set(jnp.arange(t * TOP_K, dtype=jnp.int32) // TOP_K)
    tile_start = jnp.arange(n_tiles, dtype=jnp.int32) * tm
    tile_valid = (tile_start < ends[-1]).astype(jnp.int32)
    tile_expert = jnp.minimum(jnp.sum((tile_start[:, None] >= ends[None, :]).astype(jnp.int32), axis=1),
                              N_EXPERTS - 1)
    last_expert = tile_expert[jnp.maximum(ends[-1] // tm - 1, 0)]
    tile_expert = jnp.where(tile_valid > 0, tile_expert, last_expert).astype(jnp.int32)
    xs = _gather_rows(hn, token_of)
    ys = _experts(xs, tile_expert, tile_valid, w1, w3, w2, tm)
    pos = pos.reshape(t, TOP_K).astype(jnp.int32)
    return _combine(h, gates, ys, pos[:, 0], pos[:, 1])


def _permute_in_cols(w):
    o = V_END
    wd = RWKV_WIDTH
    r = (o, o + wd)
    xw = (r[1], r[1] + RWKV_DECAY_LORA)
    k = (xw[1], xw[1] + wd)
    v = (k[1], k[1] + wd)
    xa = (v[1], v[1] + RWKV_A_LORA)
    xg = (xa[1], xa[1] + RWKV_GATE_LORA)
    order = [(0, o), r, k, v, xg, xw, xa, (RWKV_END, IN_COLS)]
    return jnp.concatenate([w[..., a:b] for a, b in order], axis=-1)


def kernel(x, norm1_g, w_in, q_norm_g, k_norm_g, attn_sinks, rel_bias, rwkv_mu, rwkv_w0, rwkv_w2, rwkv_a0, rwkv_a2, rwkv_g2, rwkv_k_k, rwkv_k_a, rwkv_r_k, rwkv_ln_w, rwkv_ln_b, ssm_lambda_re, ssm_lambda_im, ssm_b_re, ssm_b_im, ssm_c_re, ssm_c_im, ssm_d, ssm_log_dt, ssm_glu_w, ssm_glu_b, attn_out_g, ssm_out_g, w_out, norm2_g, ffn_w1, ffn_w3, ffn_w2, moe_router, moe_w1, moe_w3, moe_w2):
    batch, seq, _ = x.shape
    depth = w_in.shape[0]
    h = x.reshape(batch * seq, D_MODEL)
    for i in range(depth):
        w_in_i = _permute_in_cols(w_in[i]).astype(BF16)
        mu_i = _permute_in_cols(jnp.pad(rwkv_mu[i], (V_END, SSM_WIDTH)))[V_END:RWKV_END]
        q, kv, zr, zs = _in_proj(h, norm1_g[i], w_in_i)
        attn = _attention(q, kv, batch, seq, q_norm_g[i], k_norm_g[i], attn_sinks[i], rel_bias,
                          attn_out_g[i])
        rw = _rwkv(zr, batch, seq, mu_i, rwkv_w0[i], rwkv_w2[i], rwkv_a0[i], rwkv_a2[i], rwkv_g2[i],
                   rwkv_k_k[i], rwkv_k_a[i], rwkv_r_k[i], rwkv_ln_w[i], rwkv_ln_b[i])
        ss = _s5(zs, batch, seq, ssm_lambda_re[i], ssm_lambda_im[i], ssm_b_re[i], ssm_b_im[i],
                 ssm_c_re[i], ssm_c_im[i], ssm_d[i], ssm_log_dt[i], ssm_glu_w[i], ssm_glu_b[i],
                 ssm_out_g[i])
        j = i // 2
        if i % 2 == 0:
            h, hn = _out_proj(h, attn, rw, ss, w_out[i].astype(BF16), norm2_g[i])
            h = _ffn(h, hn, ffn_w1[j].astype(BF16), ffn_w3[j].astype(BF16), ffn_w2[j].astype(BF16))
        else:
            h, hn, idx, gates = _out_proj(h, attn, rw, ss, w_out[i].astype(BF16), norm2_g[i],
                                          router=moe_router[j])
            h = _moe(h, hn, idx, gates, moe_w1[j].astype(BF16), moe_w3[j].astype(BF16),
                     moe_w2[j].astype(BF16))
    return h.reshape(batch, seq, D_MODEL)
```

```python
import functools
import math

import numpy as np
import jax
import jax.numpy as jnp
from jax import lax
from jax.experimental import pallas as pl
from jax.experimental.pallas import tpu as pltpu

F32 = jnp.float32
BF16 = jnp.bfloat16

D_MODEL = 2048
HEAD_DIM = 64
ATTN_WIDTH = 1024
RWKV_WIDTH = 512
SSM_WIDTH = 512
ATTN_HEADS = 16
ATTN_KV_HEADS = 4
ATTN_GROUP = 4
KV_WIDTH = 256
WINDOW = 128
BLOCK = 128
REL_BUCKETS = 32
REL_MAX_DISTANCE = 128
RWKV_HEADS = 8
RWKV_DECAY_LORA = 64
RWKV_A_LORA = 64
RWKV_GATE_LORA = 128
RWKV_COLS = 1792
RWKV_DECAY_SCALE = math.exp(-0.5)
RWKV_LN_EPS = HEAD_DIM * 1e-5
SSM_GROUP_CH = 16
SSM_GROUPS = 32
SSM_STATE = 64
SSM_STATES = SSM_GROUPS * SSM_STATE
Q_END = ATTN_WIDTH
K_END = Q_END + KV_WIDTH
V_END = K_END + KV_WIDTH
RWKV_END = V_END + RWKV_COLS
IN_COLS = RWKV_END + SSM_WIDTH
D_FF = 5632
N_EXPERTS = 8
TOP_K = 2
D_FF_EXPERT = 7168
NORM_EPS = 1e-6

LANES = 128
MXU_DIM = 256
VMEM_LIMIT = 56 * 1024 * 1024

ROW_TILE = 512
FFN_COL_TILE = 512
EXPERT_COL_TILE = 1024
SCAN_TILE = 256
S5_SEGMENTS = 8
S5_STRIP = 512
RWKV_CHUNK = 64
RWKV_QUAD = 4
GATHER_TILE = 512
GATHER_UNROLL = 8
ROUTER_LANES = 128


def _cparams(sem):
    return pltpu.CompilerParams(dimension_semantics=sem, vmem_limit_bytes=VMEM_LIMIT)


def _mm(a, b):
    return jnp.dot(a.astype(BF16), b.astype(BF16), preferred_element_type=F32)


def _mm_nt(a, b):
    return lax.dot_general(a.astype(BF16), b.astype(BF16), (((1,), (1,)), ((), ())),
                           preferred_element_type=F32)


def _mm_tn(a, b):
    return lax.dot_general(a.astype(BF16), b.astype(BF16), (((0,), (0,)), ((), ())),
                           preferred_element_type=F32)


def _split2(x):
    hi = x.astype(BF16)
    lo = (x - hi.astype(F32)).astype(BF16)
    return hi, lo


def _split3(x):
    h1 = x.astype(BF16)
    r1 = x - h1.astype(F32)
    h2 = r1.astype(BF16)
    h3 = (r1 - h2.astype(F32)).astype(BF16)
    return h1, h2, h3


def _seg_sum(x, ones_bd):
    hi, lo = _split2(x)
    return (jnp.dot(hi, ones_bd, preferred_element_type=F32)
            + jnp.dot(lo, ones_bd, preferred_element_type=F32))


def _const_spec(shape):
    nd = len(shape)
    return pl.BlockSpec(shape, lambda *_: (0,) * nd, pipeline_mode=pl.Buffered(1))


def _in_proj_kernel(x_ref, g_ref, w_ref, q_ref, kv_ref, rw_ref, ss_ref):
    x = x_ref[...]
    ms = jnp.mean(x * x, axis=-1, keepdims=True)
    hn = (x * lax.rsqrt(ms + NORM_EPS) * g_ref[...]).astype(BF16)
    q_ref[...] = jnp.dot(hn, w_ref[:, 0:Q_END], preferred_element_type=F32)
    kv_ref[...] = jnp.dot(hn, w_ref[:, Q_END:V_END], preferred_element_type=F32)
    rw_ref[...] = jnp.dot(hn, w_ref[:, V_END:RWKV_END], preferred_element_type=F32)
    ss_ref[...] = jnp.dot(hn, w_ref[:, RWKV_END:IN_COLS], preferred_element_type=F32)


def _in_proj(h, g, w):
    t = h.shape[0]
    tm = min(ROW_TILE, t)
    row = lambda w_: pl.BlockSpec((tm, w_), lambda i: (i, 0))
    return pl.pallas_call(
        _in_proj_kernel,
        grid=(t // tm,),
        in_specs=[row(D_MODEL), _const_spec((1, D_MODEL)), _const_spec((D_MODEL, IN_COLS))],
        out_specs=[row(ATTN_WIDTH), row(2 * KV_WIDTH), row(RWKV_COLS), row(SSM_WIDTH)],
        out_shape=[jax.ShapeDtypeStruct((t, ATTN_WIDTH), F32),
                   jax.ShapeDtypeStruct((t, 2 * KV_WIDTH), F32),
                   jax.ShapeDtypeStruct((t, RWKV_COLS), F32),
                   jax.ShapeDtypeStruct((t, SSM_WIDTH), F32)],
        compiler_params=_cparams(("arbitrary",)),
        name="in_proj",
    )(h, g.reshape(1, D_MODEL), w)


def _attn_kernel(sink_ref, q_ref, kvp_ref, kvc_ref, bias_ref, qg_ref, kg_ref, og_ref, e_ref, dup_ref,
                 o_ref, acc_ref):
    first = pl.program_id(1) == 0
    e = e_ref[...]
    kv = jnp.concatenate([kvp_ref[...], kvc_ref[...]], axis=0)
    rows = ATTN_GROUP * BLOCK

    def head_norm(t, g):
        ss = _seg_sum(t * t, e)
        return t * lax.rsqrt(ss * (1.0 / HEAD_DIM) + NORM_EPS) * g

    kn = head_norm(kv[:, :KV_WIDTH], kg_ref[...]).astype(BF16)
    v_dup = jnp.dot(kv[:, KV_WIDTH:].astype(BF16), dup_ref[...], preferred_element_type=F32).astype(BF16)
    ones = jnp.ones((2 * BLOCK, LANES), BF16)
    key_idx = lax.broadcasted_iota(jnp.int32, (rows, 2 * BLOCK), 1)
    no_prev = jnp.logical_and(first, key_idx < BLOCK)
    group = lax.broadcasted_iota(jnp.int32, (rows, 1), 0) // BLOCK
    low_half = lax.broadcasted_iota(jnp.int32, (1, LANES), 1) < HEAD_DIM
    scale = HEAD_DIM ** -0.5

    def scores(h):
        qn = (head_norm(q_ref[:, h * MXU_DIM:(h + 1) * MXU_DIM], qg_ref[...]) * scale).astype(BF16)
        qs = jnp.concatenate([qn[:, g * HEAD_DIM:(g + 1) * HEAD_DIM] for g in range(ATTN_GROUP)], axis=0)
        bias = bias_ref[h * ATTN_GROUP:(h + 1) * ATTN_GROUP].reshape(rows, 2 * BLOCK)
        return jnp.where(no_prev, -jnp.inf, _mm_nt(qs, kn[:, h * HEAD_DIM:(h + 1) * HEAD_DIM]) + bias)

    def finish(h, s):
        sink = jnp.full((rows, 1), sink_ref[h * ATTN_GROUP], F32)
        for g in range(1, ATTN_GROUP):
            sink = jnp.where(group == g, sink_ref[h * ATTN_GROUP + g], sink)
        m = jnp.maximum(jnp.max(s, axis=-1, keepdims=True), sink)
        p = jnp.exp(s - m).astype(BF16)
        rhs = jnp.concatenate([v_dup[:, h * LANES:(h + 1) * LANES], ones], axis=1)
        pv = jnp.dot(p, rhs, preferred_element_type=F32)
        out = pv[:, :LANES] / (pv[:, LANES:] + jnp.exp(sink - m))
        for pair in range(ATTN_GROUP // 2):
            even = out[(2 * pair) * BLOCK:(2 * pair + 1) * BLOCK]
            odd = out[(2 * pair + 1) * BLOCK:(2 * pair + 2) * BLOCK]
            col = (h * ATTN_GROUP + 2 * pair) * HEAD_DIM
            acc_ref[:, col:col + LANES] = jnp.where(low_half, even, odd)

    pending = scores(0)
    for h in range(ATTN_KV_HEADS):
        nxt = scores(h + 1) if h + 1 < ATTN_KV_HEADS else None
        finish(h, pending)
        pending = nxt
    o = acc_ref[...]
    ms = jnp.mean(o * o, axis=-1, keepdims=True)
    o_ref[...] = (o * lax.rsqrt(ms + NORM_EPS) * og_ref[...]).astype(BF16)


def _t5_bucket_table():
    qi = np.arange(BLOCK)[:, None]
    kj = np.arange(2 * BLOCK)[None, :]
    dist = BLOCK + qi - kj
    max_exact = REL_BUCKETS // 2
    d = np.maximum(dist, 0)
    large = max_exact + (np.log(np.maximum(d, 1).astype(np.float32) / max_exact)
                         / math.log(REL_MAX_DISTANCE / max_exact)
                         * (REL_BUCKETS - max_exact)).astype(np.int32)
    large = np.minimum(large, REL_BUCKETS - 1)
    bucket = np.where(d < max_exact, d, large)
    valid = (dist >= 0) & (dist < WINDOW)
    return bucket, valid


def _attention(q, kv, batch, seq, q_g, k_g, sinks, rel_bias, out_g):
    nb = seq // BLOCK
    bucket, valid = _t5_bucket_table()
    bias = jnp.transpose(rel_bias.astype(F32)[bucket], (2, 0, 1))
    bias = jnp.where(valid[None], bias, -jnp.inf)
    seg = np.arange(MXU_DIM) // HEAD_DIM
    ones_bd = jnp.asarray(seg[:, None] == seg[None, :], dtype=BF16)
    dup_src = (np.arange(2 * KV_WIDTH) // LANES) * HEAD_DIM + np.arange(2 * KV_WIDTH) % HEAD_DIM
    dup = jnp.asarray(np.arange(KV_WIDTH)[:, None] == dup_src[None, :], dtype=BF16)
    tile4 = lambda g: jnp.tile(g.astype(F32), ATTN_GROUP).reshape(1, MXU_DIM)
    grid_spec = pltpu.PrefetchScalarGridSpec(
        num_scalar_prefetch=1,
        grid=(batch, nb),
        in_specs=[
            pl.BlockSpec((BLOCK, ATTN_WIDTH), lambda b, n, s: (b * nb + n, 0)),
            pl.BlockSpec((BLOCK, 2 * KV_WIDTH), lambda b, n, s: (jnp.maximum(b * nb + n - 1, 0), 0)),
            pl.BlockSpec((BLOCK, 2 * KV_WIDTH), lambda b, n, s: (b * nb + n, 0)),
            pl.BlockSpec((ATTN_HEADS, BLOCK, 2 * BLOCK), lambda b, n, s: (0, 0, 0)),
            pl.BlockSpec((1, MXU_DIM), lambda b, n, s: (0, 0)),
            pl.BlockSpec((1, MXU_DIM), lambda b, n, s: (0, 0)),
            pl.BlockSpec((1, ATTN_WIDTH), lambda b, n, s: (0, 0)),
            pl.BlockSpec((MXU_DIM, MXU_DIM), lambda b, n, s: (0, 0)),
            pl.BlockSpec((KV_WIDTH, 2 * KV_WIDTH), lambda b, n, s: (0, 0)),
        ],
        out_specs=pl.BlockSpec((BLOCK, ATTN_WIDTH), lambda b, n, s: (b * nb + n, 0)),
        scratch_shapes=[pltpu.VMEM((BLOCK, ATTN_WIDTH), F32)],
    )
    return pl.pallas_call(
        _attn_kernel,
        grid_spec=grid_spec,
        out_shape=jax.ShapeDtypeStruct((batch * seq, ATTN_WIDTH), BF16),
        compiler_params=_cparams(("arbitrary", "arbitrary")),
        name="swa",
    )(sinks.astype(F32), q, kv, kv, bias, tile4(q_g), tile4(k_g), out_g.reshape(1, ATTN_WIDTH), ones_bd,
      dup)


def _rwkv_consts():
    c, g_heads = RWKV_CHUNK, RWKV_QUAD
    w = g_heads * HEAD_DIM
    row = lax.broadcasted_iota(jnp.int32, (c, g_heads * c), 0)
    col = lax.broadcasted_iota(jnp.int32, (c, g_heads * c), 1) % c
    strict = col < row
    eye = (row == col).astype(F32)
    diag2 = jnp.logical_and((row // 2) == (col // 2), strict)
    levels = []
    bs = 2
    while bs < c:
        levels.append(((row // (2 * bs)) == (col // (2 * bs))) & ((row // bs) % 2 == 1)
                      & ((col // bs) % 2 == 0))
        bs *= 2
    hrow = lax.broadcasted_iota(jnp.int32, (w, w), 0) // HEAD_DIM
    hcol = lax.broadcasted_iota(jnp.int32, (w, w), 1) // HEAD_DIM
    return strict, col <= row, eye, diag2, levels, hrow == hcol


def _rwkv_stack(x, bm):
    return jnp.tile(x.astype(BF16), (RWKV_QUAD, 1)) * bm


def _rwkv_prepare(items, bm, consts):
    strict, incl, eye, diag2, levels, _ = consts
    c = RWKV_CHUNK
    gc = RWKV_QUAD * c
    nt = (((1,), (1,)), ((), ()))
    out = []
    for (r, lw, cum, k, v, a, b) in items:
        last = cum[c - 1:c]
        p_inv = jnp.exp(-cum)
        ratio = jnp.exp(last - cum)
        out.append(dict(
            ar=jnp.concatenate([a * jnp.exp(cum - lw), r * jnp.exp(cum)], axis=0).astype(BF16),
            bks=jnp.concatenate([_rwkv_stack(b * p_inv, bm), _rwkv_stack(k * p_inv, bm)], axis=0),
            v=v, v_stack=_rwkv_stack(v, bm),
            uv_rhs=jnp.concatenate([b * ratio, k * ratio], axis=0).astype(BF16),
            p_last=jnp.exp(last)))
    bigs = [lax.dot_general(d["ar"], d.pop("bks"), nt, preferred_element_type=F32) for d in out]
    n_cats, t_invs = [], []
    for d, big in zip(out, bigs):
        n_cat = jnp.where(strict, big[:c, :gc], 0.0)
        n_cats.append(n_cat)
        t_invs.append(eye + jnp.where(diag2, n_cat, 0.0))
        d["r_b"] = jnp.where(incl, big[c:, :gc], 0.0).astype(BF16)
        d["akrk"] = jnp.concatenate([jnp.where(strict, big[:c, gc:], 0.0),
                                     jnp.where(incl, big[c:, gc:], 0.0)], axis=0).astype(BF16)
    for lm in levels:
        tns = [_mm(t, _rwkv_stack(jnp.where(lm, n, 0.0), bm)) for t, n in zip(t_invs, n_cats)]
        t_invs = [t + _mm(tn, _rwkv_stack(t, bm)) for t, tn in zip(t_invs, tns)]
    for d, t in zip(out, t_invs):
        d["t_inv"] = t.astype(BF16)
        d["wy"] = jnp.dot(d.pop("akrk"), d.pop("v_stack"), preferred_element_type=F32)
    return out


def _rwkv_apply(ds, hts, bm, consts):
    c = RWKV_CHUNK
    nt = (((1,), (1,)), ((), ()))
    ahs = [lax.dot_general(d["ar"], ht.astype(BF16), nt, preferred_element_type=F32)
           for d, ht in zip(ds, hts)]
    us = [_mm(d["t_inv"], _rwkv_stack(ah[:c] + d["wy"][:c], bm)) for d, ah in zip(ds, ahs)]
    upds = [_mm_tn(jnp.concatenate([u, d["v"]], axis=0), d["uv_rhs"]) for d, u in zip(ds, us)]
    ys = [ah[c:] + _mm(d["r_b"], _rwkv_stack(u, bm)) + d["wy"][c:] for d, ah, u in zip(ds, ahs, us)]
    new = [ht * d["p_last"] + jnp.where(consts[5], upd, 0.0) for d, ht, upd in zip(ds, hts, upds)]
    return ys, new


def _rwkv_kernel(z_ref, mu_ref, w2a_ref, g2_ref, w0_ref, a0_ref, kk_ref, ka_ref, rk_ref, lnw_ref,
                 lnb_ref, e_ref, tri_ref, bm_ref, o_ref, carry_ref, y_s, h_s):
    tb = z_ref.shape[0]
    wd = RWKV_WIDTH
    quad_w = RWKV_QUAD * HEAD_DIM
    n_quads = RWKV_HEADS // RWKV_QUAD
    c = RWKV_CHUNK

    @pl.when(pl.program_id(1) == 0)
    def _():
        carry_ref[...] = jnp.zeros_like(carry_ref)
        h_s[...] = jnp.zeros_like(h_s)

    z = z_ref[...]
    rows = lax.broadcasted_iota(jnp.int32, (tb, 1), 0)
    prev = jnp.where(rows == 0, carry_ref[...], pltpu.roll(z, 1, axis=0))
    carry_ref[...] = z[tb - 1:tb]
    zs = z + mu_ref[...] * (prev - z)
    r = zs[:, 0:wd]
    k = zs[:, wd:2 * wd]
    v = zs[:, 2 * wd:3 * wd]
    xg = zs[:, 3 * wd:3 * wd + RWKV_GATE_LORA]
    xwa = zs[:, 3 * wd + RWKV_GATE_LORA:]
    lane = lax.broadcasted_iota(jnp.int32, (1, RWKV_DECAY_LORA + RWKV_A_LORA), 1)
    lora_in = jnp.where(lane < RWKV_DECAY_LORA, jnp.tanh(xwa), xwa)
    lora = _mm(lora_in, w2a_ref[...])
    lw = -RWKV_DECAY_SCALE * jax.nn.sigmoid(w0_ref[...] + lora[:, :wd])
    iclr = jax.nn.sigmoid(a0_ref[...] + lora[:, wd:])
    gate = _mm(jax.nn.sigmoid(xg), g2_ref[...])
    kk = k * kk_ref[...]
    e = e_ref[...]
    seg = lambda x: jnp.concatenate(
        [_seg_sum(x[:, q * quad_w:(q + 1) * quad_w], e) for q in range(n_quads)], axis=1)
    kk = kk * lax.rsqrt(jnp.maximum(seg(kk * kk), 1e-12))
    k2 = k * (1.0 + (iclr - 1.0) * ka_ref[...])
    l1, l2, l3 = _split3(lw)
    tri = tri_ref[...]
    cum = (jnp.dot(tri, l1, preferred_element_type=F32) + jnp.dot(tri, l2, preferred_element_type=F32)
           + jnp.dot(tri, l3, preferred_element_type=F32))
    a = -kk
    b = kk * iclr

    consts = _rwkv_consts()
    bm = bm_ref[...]
    n_chunks = tb // c
    items = []
    for ci in range(n_chunks):
        for q in range(n_quads):
            sl = (slice(ci * c, (ci + 1) * c), slice(q * quad_w, (q + 1) * quad_w))
            items.append((r[sl], lw[sl], cum[sl], k2[sl], v[sl], a[sl], b[sl]))
    prepared = _rwkv_prepare(items, bm, consts)
    hts = [h_s[q] for q in range(n_quads)]
    for ci in range(n_chunks):
        ys, hts = _rwkv_apply(prepared[ci * n_quads:(ci + 1) * n_quads], hts, bm, consts)
        for q in range(n_quads):
            y_s[ci * c:(ci + 1) * c, q * quad_w:(q + 1) * quad_w] = ys[q]
    for q in range(n_quads):
        h_s[q] = hts[q]

    y = y_s[...]
    mean = seg(y) * (1.0 / HEAD_DIM)
    d = y - mean
    var = seg(d * d) * (1.0 / HEAD_DIM)
    yn = d * lax.rsqrt(var + RWKV_LN_EPS) * lnw_ref[...] + lnb_ref[...]
    bonus = seg(r * k2 * rk_ref[...]) * v
    o_ref[...] = ((yn + bonus) * gate).astype(BF16)


def _rwkv(z, batch, seq, mu, w0, w2, a0, a2, g2, k_k, k_a, r_k, ln_w, ln_b):
    tb = min(SCAN_TILE, seq)
    nb = seq // tb
    wd = RWKV_WIDTH
    quad_w = RWKV_QUAD * HEAD_DIM
    w2a = jnp.zeros((RWKV_DECAY_LORA + RWKV_A_LORA, 2 * wd), F32)
    w2a = w2a.at[:RWKV_DECAY_LORA, :wd].set(w2).at[RWKV_DECAY_LORA:, wd:].set(a2).astype(BF16)
    seg = np.arange(quad_w) // HEAD_DIM
    ones_bd = jnp.asarray(seg[:, None] == seg[None, :], dtype=BF16)
    t_idx = np.arange(tb)
    tri = jnp.asarray((t_idx[:, None] // RWKV_CHUNK == t_idx[None, :] // RWKV_CHUNK)
                      & (t_idx[None, :] <= t_idx[:, None]), dtype=BF16)
    stack_rows = np.arange(RWKV_QUAD * RWKV_CHUNK) // RWKV_CHUNK
    stack_mask = jnp.asarray(stack_rows[:, None] == seg[None, :], dtype=BF16)
    vec = lambda p: p.astype(F32).reshape(1, -1)
    return pl.pallas_call(
        _rwkv_kernel,
        grid=(batch, nb),
        in_specs=[pl.BlockSpec((tb, RWKV_COLS), lambda b, j: (b * nb + j, 0)),
                  _const_spec((1, RWKV_COLS)),
                  _const_spec((RWKV_DECAY_LORA + RWKV_A_LORA, 2 * wd)),
                  _const_spec((RWKV_GATE_LORA, wd))]
                 + [_const_spec((1, wd))] * 7
                 + [_const_spec((quad_w, quad_w)), _const_spec((tb, tb)),
                    _const_spec((RWKV_QUAD * RWKV_CHUNK, quad_w))],
        out_specs=pl.BlockSpec((tb, wd), lambda b, j: (b * nb + j, 0)),
        out_shape=jax.ShapeDtypeStruct((batch * seq, wd), BF16),
        scratch_shapes=[pltpu.VMEM((1, RWKV_COLS), F32), pltpu.VMEM((tb, wd), F32),
                        pltpu.VMEM((RWKV_HEADS // RWKV_QUAD, quad_w, quad_w), F32)],
        compiler_params=_cparams(("arbitrary", "arbitrary")),
        name="rwkv7",
    )(z, vec(mu), w2a, g2.astype(BF16), vec(w0), vec(a0), vec(k_k), vec(k_a), vec(r_k), vec(ln_w),
      vec(ln_b), ones_bd, tri, stack_mask)


def _cmul_add(ar, ai, xr, xi, br, bi):
    return ar * xr - ai * xi + br, ar * xi + ai * xr + bi


def _s5_kernel(u_ref, perm_ref, bbr_ref, bbi_ref, cr_ref, ci_ref, a1_ref, aseg_ref, apj_ref, ask_ref,
               d_ref, gw_ref, gb_ref, og_ref, o_ref, xr_s, xi_s, car_s, cai_s):
    tb = u_ref.shape[0]
    seg_len = tb // S5_SEGMENTS
    strip = S5_STRIP
    n_strips = SSM_STATES // strip
    n_groups = SSM_WIDTH // LANES

    @pl.when(pl.program_id(1) == 0)
    def _():
        car_s[...] = jnp.zeros_like(car_s)
        cai_s[...] = jnp.zeros_like(cai_s)

    perm = perm_ref[...]
    u1, u2, u3 = _split3(u_ref[...])
    u = (jnp.dot(perm, u1, preferred_element_type=F32) + jnp.dot(perm, u2, preferred_element_type=F32)
         + jnp.dot(perm, u3, preferred_element_type=F32))
    ub = u.astype(BF16)
    sw = SSM_STATES // n_groups
    for g in range(n_groups):
        ug = ub[:, g * LANES:(g + 1) * LANES]
        xr_s[:, g * sw:(g + 1) * sw] = jnp.dot(ug, bbr_ref[g * LANES:(g + 1) * LANES, g * sw:(g + 1) * sw],
                                               preferred_element_type=F32)
        xi_s[:, g * sw:(g + 1) * sw] = jnp.dot(ug, bbi_ref[g * LANES:(g + 1) * LANES, g * sw:(g + 1) * sw],
                                               preferred_element_type=F32)

    for s in range(n_strips):
        cols = slice(s * strip, (s + 1) * strip)
        ar = jnp.broadcast_to(a1_ref[0:1, cols], (S5_SEGMENTS, strip))
        ai = jnp.broadcast_to(a1_ref[1:2, cols], (S5_SEGMENTS, strip))

        def scan_body(j, x):
            rs = pl.ds(pl.multiple_of(j * S5_SEGMENTS, S5_SEGMENTS), S5_SEGMENTS)
            nr, ni = _cmul_add(ar, ai, x[0], x[1], xr_s[rs, cols], xi_s[rs, cols])
            xr_s[rs, cols] = nr
            xi_s[rs, cols] = ni
            return nr, ni

        fr, fi = lax.fori_loop(1, seg_len, scan_body,
                               (xr_s[0:S5_SEGMENTS, cols], xi_s[0:S5_SEGMENTS, cols]), unroll=2)
        seg_row = lax.broadcasted_iota(jnp.int32, (S5_SEGMENTS, strip), 0)
        ir, ii = fr, fi
        for lvl in range(int(math.log2(S5_SEGMENTS))):
            sh = 1 << lvl
            pr = jnp.where(seg_row >= sh, pltpu.roll(ir, sh, axis=0), 0.0)
            pi = jnp.where(seg_row >= sh, pltpu.roll(ii, sh, axis=0), 0.0)
            ir, ii = _cmul_add(aseg_ref[2 * lvl:2 * lvl + 1, cols], aseg_ref[2 * lvl + 1:2 * lvl + 2, cols],
                               pr, pi, ir, ii)
        c_in_r = car_s[:, cols]
        c_in_i = cai_s[:, cols]
        er = jnp.where(seg_row >= 1, pltpu.roll(ir, 1, axis=0), 0.0)
        ei = jnp.where(seg_row >= 1, pltpu.roll(ii, 1, axis=0), 0.0)
        cr_, ci_ = _cmul_add(ask_ref[0:S5_SEGMENTS, cols], ask_ref[S5_SEGMENTS:2 * S5_SEGMENTS, cols],
                             c_in_r, c_in_i, er, ei)
        last = S5_SEGMENTS - 1
        nr, ni = _cmul_add(aseg_ref[0:1, cols], aseg_ref[1:2, cols],
                           cr_[last:last + 1], ci_[last:last + 1], fr[last:last + 1], fi[last:last + 1])
        car_s[:, cols] = nr
        cai_s[:, cols] = ni

        def fix_body(j, carry):
            rs = pl.ds(pl.multiple_of(j * S5_SEGMENTS, S5_SEGMENTS), S5_SEGMENTS)
            pr = apj_ref[pl.ds(2 * j, 1), cols]
            pi = apj_ref[pl.ds(2 * j + 1, 1), cols]
            nr, ni = _cmul_add(pr, pi, cr_, ci_, xr_s[rs, cols], xi_s[rs, cols])
            xr_s[rs, cols] = nr
            xi_s[rs, cols] = ni
            return carry

        lax.fori_loop(0, seg_len, fix_body, 0, unroll=2)

    sg = SSM_STATES // n_groups
    ys = []
    for g in range(n_groups):
        xr = xr_s[:, g * sg:(g + 1) * sg].astype(BF16)
        xi = xi_s[:, g * sg:(g + 1) * sg].astype(BF16)
        ys.append(jnp.dot(xr, cr_ref[g * sg:(g + 1) * sg, g * LANES:(g + 1) * LANES], preferred_element_type=F32)
                  - jnp.dot(xi, ci_ref[g * sg:(g + 1) * sg, g * LANES:(g + 1) * LANES], preferred_element_type=F32))
    y = jnp.concatenate(ys, axis=1) + d_ref[...] * u
    zg = jax.nn.gelu(y)
    gl = _mm(zg, gw_ref[...]) + gb_ref[...]
    out = zg * jax.nn.sigmoid(gl)
    ms = jnp.mean(out * out, axis=-1, keepdims=True)
    out = (out * lax.rsqrt(ms + NORM_EPS) * og_ref[...]).astype(BF16)
    o_ref[...] = lax.dot_general(perm, out, (((0,), (0,)), ((), ())),
                                 preferred_element_type=F32).astype(BF16)


def _s5(u, batch, seq, lam_re, lam_im, b_re, b_im, c_re, c_im, d, log_dt, glu_w, glu_b, out_g):
    tb = min(SCAN_TILE, seq)
    nb = seq // tb
    seg_len = tb // S5_SEGMENTS
    f = lambda p: p.astype(F32)
    lr, li = f(lam_re), f(lam_im)
    dt = jnp.exp(f(log_dt))[:, None]
    mag = jnp.exp(lr * dt)
    ab_re = mag * jnp.cos(li * dt)
    ab_im = mag * jnp.sin(li * dt)
    den = lr * lr + li * li
    fr = ((ab_re - 1.0) * lr + ab_im * li) / den
    fi = (ab_im * lr - (ab_re - 1.0) * li) / den
    bb_re = fr[..., None] * f(b_re) - fi[..., None] * f(b_im)
    bb_im = fr[..., None] * f(b_im) + fi[..., None] * f(b_re)
    eye = jnp.eye(SSM_GROUPS, dtype=F32)
    in_bd = lambda t: jnp.einsum('gpm,gh->gmhp', t, eye).reshape(SSM_WIDTH, SSM_STATES).astype(BF16)
    out_bd = lambda t: jnp.einsum('gmp,gh->gphm', f(t), eye).reshape(SSM_STATES, SSM_WIDTH).astype(BF16)

    def a_pow(n):
        n = jnp.asarray(n, F32)[:, None, None]
        m = jnp.exp(n * (lr * dt))
        ang = n * (li * dt)
        both = jnp.stack([m * jnp.cos(ang), m * jnp.sin(ang)], axis=1)
        return both.reshape(-1, SSM_STATES)

    a1 = a_pow([1])
    aseg = a_pow([seg_len * (1 << l) for l in range(4)])
    apj = a_pow(np.arange(1, seg_len + 1))
    ask_both = a_pow(np.arange(S5_SEGMENTS) * seg_len).reshape(S5_SEGMENTS, 2, SSM_STATES)
    ask = jnp.concatenate([ask_both[:, 0], ask_both[:, 1]], axis=0)
    t_idx = np.arange(tb)
    src_time = (t_idx % S5_SEGMENTS) * seg_len + t_idx // S5_SEGMENTS
    perm = jnp.asarray(src_time[:, None] == t_idx[None, :], dtype=BF16)
    vec = lambda p: f(p).reshape(1, -1)
    buf = lambda: pltpu.VMEM((tb, SSM_STATES), F32)
    return pl.pallas_call(
        _s5_kernel,
        grid=(batch, nb),
        in_specs=[pl.BlockSpec((tb, SSM_WIDTH), lambda b, j: (b * nb + j, 0)),
                  _const_spec((tb, tb)),
                  _const_spec((SSM_WIDTH, SSM_STATES)), _const_spec((SSM_WIDTH, SSM_STATES)),
                  _const_spec((SSM_STATES, SSM_WIDTH)), _const_spec((SSM_STATES, SSM_WIDTH)),
                  _const_spec((2, SSM_STATES)), _const_spec((8, SSM_STATES)),
                  _const_spec((2 * seg_len, SSM_STATES)), _const_spec((2 * S5_SEGMENTS, SSM_STATES)),
                  _const_spec((1, SSM_WIDTH)), _const_spec((SSM_WIDTH, SSM_WIDTH)),
                  _const_spec((1, SSM_WIDTH)), _const_spec((1, SSM_WIDTH))],
        out_specs=pl.BlockSpec((tb, SSM_WIDTH), lambda b, j: (b * nb + j, 0)),
        out_shape=jax.ShapeDtypeStruct((batch * seq, SSM_WIDTH), BF16),
        scratch_shapes=[buf(), buf(),
                        pltpu.VMEM((1, SSM_STATES), F32), pltpu.VMEM((1, SSM_STATES), F32)],
        compiler_params=_cparams(("arbitrary", "arbitrary")),
        name="s5",
    )(u, perm, in_bd(bb_re), in_bd(bb_im), out_bd(c_re), out_bd(c_im), a1, aseg, apj, ask, vec(d),
      glu_w.astype(BF16), vec(glu_b), vec(out_g))


def _out_proj_kernel(h_ref, at_ref, rw_ref, ss_ref, w_ref, g_ref, *rest, with_router):
    if with_router:
        rt_ref, ho_ref, hn_ref, idx_ref, gate_ref = rest
    else:
        ho_ref, hn_ref = rest
    mix = (jnp.dot(at_ref[...], w_ref[0:ATTN_WIDTH, :], preferred_element_type=F32)
           + jnp.dot(rw_ref[...], w_ref[ATTN_WIDTH:ATTN_WIDTH + RWKV_WIDTH, :], preferred_element_type=F32)
           + jnp.dot(ss_ref[...], w_ref[ATTN_WIDTH + RWKV_WIDTH:, :], preferred_element_type=F32))
    h = h_ref[...] + mix
    ho_ref[...] = h
    ms = jnp.mean(h * h, axis=-1, keepdims=True)
    hn = h * lax.rsqrt(ms + NORM_EPS) * g_ref[...]
    if not with_router:
        hn_ref[...] = hn.astype(BF16)
        return
    hn_ref[...] = hn
    h_hi, h_lo = _split2(hn)
    r_hi, r_lo = _split2(rt_ref[...])
    logits = (jnp.dot(h_hi, r_hi, preferred_element_type=F32)
              + jnp.dot(h_lo, r_hi, preferred_element_type=F32)
              + jnp.dot(h_hi, r_lo, preferred_element_type=F32))
    lane_i = lax.broadcasted_iota(jnp.int32, logits.shape, 1)
    lane = lane_i.astype(F32)
    logits = jnp.where(lane_i < N_EXPERTS, logits, -jnp.inf)
    m1 = jnp.max(logits, axis=-1, keepdims=True)
    i1 = jnp.min(jnp.where(logits == m1, lane, float(ROUTER_LANES)), axis=-1, keepdims=True)
    rest_l = jnp.where(lane == i1, -jnp.inf, logits)
    m2 = jnp.max(rest_l, axis=-1, keepdims=True)
    i2 = jnp.min(jnp.where(rest_l == m2, lane, float(ROUTER_LANES)), axis=-1, keepdims=True)
    e2 = jnp.exp(m2 - m1)
    g1 = 1.0 / (1.0 + e2)
    g2 = e2 / (1.0 + e2)
    idx_ref[...] = jnp.where(lane_i == 0, i1, jnp.where(lane_i == 1, i2, 0.0)).astype(jnp.int32)
    gate_ref[...] = jnp.where(lane_i == 0, g1, jnp.where(lane_i == 1, g2, 0.0))


def _out_proj(h, attn, rw, ss, w_out, g2, router=None):
    t = h.shape[0]
    tm = min(ROW_TILE, t)
    row = lambda w_: pl.BlockSpec((tm, w_), lambda i: (i, 0))
    with_router = router is not None
    in_specs = [row(D_MODEL), row(ATTN_WIDTH), row(RWKV_WIDTH), row(SSM_WIDTH),
                _const_spec((D_MODEL, D_MODEL)), _const_spec((1, D_MODEL))]
    args = [h, attn, rw, ss, w_out, g2.reshape(1, D_MODEL)]
    out_specs = [row(D_MODEL), row(D_MODEL)]
    out_shape = [jax.ShapeDtypeStruct((t, D_MODEL), F32),
                 jax.ShapeDtypeStruct((t, D_MODEL), F32 if with_router else BF16)]
    if with_router:
        rt = jnp.zeros((D_MODEL, ROUTER_LANES), F32).at[:, :N_EXPERTS].set(router.astype(F32))
        in_specs.append(_const_spec((D_MODEL, ROUTER_LANES)))
        args.append(rt)
        out_specs += [row(ROUTER_LANES), row(ROUTER_LANES)]
        out_shape += [jax.ShapeDtypeStruct((t, ROUTER_LANES), jnp.int32),
                      jax.ShapeDtypeStruct((t, ROUTER_LANES), F32)]
    return pl.pallas_call(
        functools.partial(_out_proj_kernel, with_router=with_router),
        grid=(t // tm,),
        in_specs=in_specs, out_specs=out_specs, out_shape=out_shape,
        compiler_params=_cparams(("arbitrary",)),
        name="out_proj_router" if with_router else "out_proj",
    )(*args)


def _ffn_kernel(h_ref, hn_ref, w1_ref, w3_ref, w2_ref, o_ref):
    f = pl.program_id(1)

    @pl.when(f == 0)
    def _():
        o_ref[...] = h_ref[...]

    hn = hn_ref[...]
    a1 = jnp.dot(hn, w1_ref[...], preferred_element_type=F32)
    a3 = jnp.dot(hn, w3_ref[...], preferred_element_type=F32)
    act = (a1 * jax.nn.sigmoid(a1) * a3).astype(BF16)
    o_ref[...] += jnp.dot(act, w2_ref[...], preferred_element_type=F32)


def _ffn(h, hn, w1, w3, w2):
    t = h.shape[0]
    tm = min(ROW_TILE, t)
    d_ff = w1.shape[1]
    tf = FFN_COL_TILE
    return pl.pallas_call(
        _ffn_kernel,
        grid=(t // tm, d_ff // tf),
        in_specs=[pl.BlockSpec((tm, D_MODEL), lambda i, f: (i, 0)),
                  pl.BlockSpec((tm, D_MODEL), lambda i, f: (i, 0)),
                  pl.BlockSpec((D_MODEL, tf), lambda i, f: (0, f)),
                  pl.BlockSpec((D_MODEL, tf), lambda i, f: (0, f)),
                  pl.BlockSpec((tf, D_MODEL), lambda i, f: (f, 0))],
        out_specs=pl.BlockSpec((tm, D_MODEL), lambda i, f: (i, 0)),
        out_shape=jax.ShapeDtypeStruct((t, D_MODEL), F32),
        compiler_params=_cparams(("arbitrary", "arbitrary")),
        name="ffn",
    )(h, hn, w1, w3, w2)


def _dispatch_kernel(starts_ref, padded_ref, used_ref, pos_ref, src_ref, zero_ref, o_ref, sem, zsem):
    i = pl.program_id(0)
    tokens = pos_ref.shape[1] // TOP_K
    tm = zero_ref.shape[0]
    n_tiles = o_ref.shape[0] // tm

    def zero_tile(row0):
        return pltpu.make_async_copy(zero_ref, o_ref.at[pl.ds(pl.multiple_of(row0, tm), tm)], zsem)

    def tail_fill(e):
        return zero_tile(starts_ref[e] + padded_ref[e] - tm)

    @pl.when(i == 0)
    def _():
        for e in range(N_EXPERTS):
            pl.when(padded_ref[e] > 0)(lambda e=e: tail_fill(e).start())
        lax.fori_loop(used_ref[0], n_tiles, lambda j, c: (zero_tile(j * tm).start(), c)[1], 0)
        for e in range(N_EXPERTS):
            pl.when(padded_ref[e] > 0)(lambda e=e: tail_fill(e).wait())
        lax.fori_loop(used_ref[0], n_tiles, lambda j, c: (zero_tile(j * tm).wait(), c)[1], 0)

    def copy(r, k):
        return pltpu.make_async_copy(src_ref.at[pl.ds(i * tokens + r, 1)],
                                     o_ref.at[pl.ds(pos_ref[0, TOP_K * r + k], 1)], sem)

    def issue(r, c):
        for k in range(TOP_K):
            copy(r, k).start()
        return c

    def drain(r, c):
        for k in range(TOP_K):
            copy(r, k).wait()
        return c

    lax.fori_loop(0, tokens, issue, 0, unroll=GATHER_UNROLL)
    lax.fori_loop(0, tokens, drain, 0, unroll=GATHER_UNROLL)


def _dispatch(src, pos, starts, padded, n_used, n_rows, tm):
    t, width = src.shape
    tg = min(GATHER_TILE, t)
    grid_spec = pltpu.PrefetchScalarGridSpec(
        num_scalar_prefetch=3,
        grid=(t // tg,),
        in_specs=[pl.BlockSpec((None, 1, TOP_K * tg), lambda i, s, p, u: (i, 0, 0), memory_space=pltpu.SMEM),
                  pl.BlockSpec(memory_space=pl.ANY),
                  pl.BlockSpec((tm, width), lambda i, s, p, u: (0, 0), pipeline_mode=pl.Buffered(1))],
        out_specs=pl.BlockSpec(memory_space=pl.ANY),
        scratch_shapes=[pltpu.SemaphoreType.DMA(()), pltpu.SemaphoreType.DMA(())],
    )
    return pl.pallas_call(
        _dispatch_kernel,
        grid_spec=grid_spec,
        out_shape=jax.ShapeDtypeStruct((n_rows, width), src.dtype),
        compiler_params=_cparams(("arbitrary",)),
        name="moe_dispatch",
    )(starts, padded, n_used, pos.reshape(t // tg, 1, TOP_K * tg), src, jnp.zeros((tm, width), src.dtype))


def _expert_kernel(te_ref, tv_ref, tx_ref, x_ref, w1_ref, w3_ref, w2_ref, o_ref, xb_ref):
    i = pl.program_id(0)
    f = pl.program_id(1)

    @pl.when(f == 0)
    def _():
        xb_ref[...] = x_ref[...].astype(BF16)
        o_ref[...] = jnp.zeros_like(o_ref)

    @pl.when(tv_ref[i] > 0)
    def _():
        xb = xb_ref[...]
        a1 = jnp.dot(xb, w1_ref[...], preferred_element_type=F32)
        a3 = jnp.dot(xb, w3_ref[...], preferred_element_type=F32)
        act = (a1 * jax.nn.sigmoid(a1) * a3).astype(BF16)
        o_ref[...] += jnp.dot(act, w2_ref[...], preferred_element_type=F32)


def _experts(xs, tile_expert, tile_valid, tile_src, w1, w3, w2, tm):
    n = xs.shape[0]
    d_ff = w1.shape[2]
    tf = EXPERT_COL_TILE
    nf = d_ff // tf
    fidx = lambda i, f, tv: jnp.where(tv[i] > 0, f, nf - 1)
    grid_spec = pltpu.PrefetchScalarGridSpec(
        num_scalar_prefetch=3,
        grid=(n // tm, nf),
        in_specs=[pl.BlockSpec((tm, D_MODEL), lambda i, f, te, tv, tx: (tx[i], 0)),
                  pl.BlockSpec((None, D_MODEL, tf), lambda i, f, te, tv, tx: (te[i], 0, fidx(i, f, tv))),
                  pl.BlockSpec((None, D_MODEL, tf), lambda i, f, te, tv, tx: (te[i], 0, fidx(i, f, tv))),
                  pl.BlockSpec((None, tf, D_MODEL), lambda i, f, te, tv, tx: (te[i], fidx(i, f, tv), 0))],
        out_specs=pl.BlockSpec((tm, D_MODEL), lambda i, f, te, tv, tx: (i, 0)),
        scratch_shapes=[pltpu.VMEM((tm, D_MODEL), BF16)],
    )
    return pl.pallas_call(
        _expert_kernel,
        grid_spec=grid_spec,
        out_shape=jax.ShapeDtypeStruct((n, D_MODEL), F32),
        compiler_params=_cparams(("arbitrary", "arbitrary")),
        name="moe_experts",
    )(tile_expert, tile_valid, tile_src, xs, w1, w3, w2)


def _combine_kernel(p0_ref, p1_ref, h_ref, gate_ref, ys_ref, o_ref, b0, b1, sem0, sem1):
    rows = o_ref.shape[0]

    def copies(r):
        return (pltpu.make_async_copy(ys_ref.at[pl.ds(p0_ref[0, r], 1)], b0.at[pl.ds(r, 1)], sem0),
                pltpu.make_async_copy(ys_ref.at[pl.ds(p1_ref[0, r], 1)], b1.at[pl.ds(r, 1)], sem1))

    def issue(r, c):
        c0, c1 = copies(r)
        c0.start()
        c1.start()
        return c

    def drain(r, c):
        c0, c1 = copies(r)
        c0.wait()
        c1.wait()
        return c

    lax.fori_loop(0, rows, issue, 0, unroll=GATHER_UNROLL)
    lax.fori_loop(0, rows, drain, 0, unroll=GATHER_UNROLL)
    gate = gate_ref[...]
    o_ref[...] = h_ref[...] + gate[:, 0:1] * b0[...] + gate[:, 1:2] * b1[...]


def _combine(h, gates, ys, pos0, pos1):
    t = h.shape[0]
    tg = min(GATHER_TILE, t)
    smem = pl.BlockSpec((None, 1, tg), lambda i: (i, 0, 0), memory_space=pltpu.SMEM)
    return pl.pallas_call(
        _combine_kernel,
        grid=(t // tg,),
        in_specs=[smem, smem,
                  pl.BlockSpec((tg, D_MODEL), lambda i: (i, 0)),
                  pl.BlockSpec((tg, ROUTER_LANES), lambda i: (i, 0)),
                  pl.BlockSpec(memory_space=pl.ANY)],
        out_specs=pl.BlockSpec((tg, D_MODEL), lambda i: (i, 0)),
        out_shape=jax.ShapeDtypeStruct((t, D_MODEL), F32),
        scratch_shapes=[pltpu.VMEM((tg, D_MODEL), F32), pltpu.VMEM((tg, D_MODEL), F32),
                        pltpu.SemaphoreType.DMA(()), pltpu.SemaphoreType.DMA(())],
        compiler_params=_cparams(("arbitrary",)),
        name="moe_combine",
    )(pos0.reshape(t // tg, 1, tg), pos1.reshape(t // tg, 1, tg), h, gates, ys)


def _moe(h, hn, idx, gates, w1, w3, w2):
    t = h.shape[0]
    tm = min(ROW_TILE, t)
    n_tiles = (t * TOP_K) // tm + N_EXPERTS
    n_rows = n_tiles * tm
    flat_e = idx[:, :TOP_K].reshape(-1)
    onehot = (flat_e[:, None] == jnp.arange(N_EXPERTS)[None, :]).astype(jnp.int32)
    rank = jnp.sum((jnp.cumsum(onehot, axis=0) - onehot) * onehot, axis=1)
    counts = jnp.sum(onehot, axis=0)
    padded = ((counts + tm - 1) // tm) * tm
    ends = jnp.cumsum(padded)
    starts = ends - padded
    pos = (starts[flat_e] + rank).astype(jnp.int32)
    tile_idx = jnp.arange(n_tiles, dtype=jnp.int32)
    tile_start = tile_idx * tm
    tile_valid = (tile_start < ends[-1]).astype(jnp.int32)
    tile_expert = jnp.minimum(jnp.sum((tile_start[:, None] >= ends[None, :]).astype(jnp.int32), axis=1),
                              N_EXPERTS - 1)
    last_tile = jnp.maximum(ends[-1] // tm - 1, 0).astype(jnp.int32)
    tile_expert = jnp.where(tile_valid > 0, tile_expert, tile_expert[last_tile]).astype(jnp.int32)
    tile_src = jnp.where(tile_valid > 0, tile_idx, last_tile)
    n_used = (ends[-1:] // tm).astype(jnp.int32)
    xs = _dispatch(hn, pos, starts.astype(jnp.int32), padded.astype(jnp.int32), n_used, n_rows, tm)
    ys = _experts(xs, tile_expert, tile_valid, tile_src, w1, w3, w2, tm)
    pos = pos.reshape(t, TOP_K)
    return _combine(h, gates, ys, pos[:, 0], pos[:, 1])


def _permute_in_cols(w):
    o = V_END
    wd = RWKV_WIDTH
    r = (o, o + wd)
    xw = (r[1], r[1] + RWKV_DECAY_LORA)
    k = (xw[1], xw[1] + wd)
    v = (k[1], k[1] + wd)
    xa = (v[1], v[1] + RWKV_A_LORA)
    xg = (xa[1], xa[1] + RWKV_GATE_LORA)
    order = [(0, o), r, k, v, xg, xw, xa, (RWKV_END, IN_COLS)]
    return jnp.concatenate([w[..., a:b] for a, b in order], axis=-1)


def kernel(x, norm1_g, w_in, q_norm_g, k_norm_g, attn_sinks, rel_bias, rwkv_mu, rwkv_w0, rwkv_w2, rwkv_a0, rwkv_a2, rwkv_g2, rwkv_k_k, rwkv_k_a, rwkv_r_k, rwkv_ln_w, rwkv_ln_b, ssm_lambda_re, ssm_lambda_im, ssm_b_re, ssm_b_im, ssm_c_re, ssm_c_im, ssm_d, ssm_log_dt, ssm_glu_w, ssm_glu_b, attn_out_g, ssm_out_g, w_out, norm2_g, ffn_w1, ffn_w3, ffn_w2, moe_router, moe_w1, moe_w3, moe_w2):
    batch, seq, _ = x.shape
    depth = w_in.shape[0]
    h = x.reshape(batch * seq, D_MODEL)
    for i in range(depth):
        w_in_i = _permute_in_cols(w_in[i]).astype(BF16)
        mu_i = _permute_in_cols(jnp.pad(rwkv_mu[i], (V_END, SSM_WIDTH)))[V_END:RWKV_END]
        q, kv, zr, zs = _in_proj(h, norm1_g[i], w_in_i)
        attn = _attention(q, kv, batch, seq, q_norm_g[i], k_norm_g[i], attn_sinks[i], rel_bias,
                          attn_out_g[i])
        rw = _rwkv(zr, batch, seq, mu_i, rwkv_w0[i], rwkv_w2[i], rwkv_a0[i], rwkv_a2[i], rwkv_g2[i],
                   rwkv_k_k[i], rwkv_k_a[i], rwkv_r_k[i], rwkv_ln_w[i], rwkv_ln_b[i])
        ss = _s5(zs, batch, seq, ssm_lambda_re[i], ssm_lambda_im[i], ssm_b_re[i], ssm_b_im[i],
                 ssm_c_re[i], ssm_c_im[i], ssm_d[i], ssm_log_dt[i], ssm_glu_w[i], ssm_glu_b[i],
                 ssm_out_g[i])
        j = i // 2
        if i % 2 == 0:
            h, hn = _out_proj(h, attn, rw, ss, w_out[i].astype(BF16), norm2_g[i])
            h = _ffn(h, hn, ffn_w1[j].astype(BF16), ffn_w3[j].astype(BF16), ffn_w2[j].astype(BF16))
        else:
            h, hn, idx, gates = _out_proj(h, attn, rw, ss, w_out[i].astype(BF16), norm2_g[i],
                                          router=moe_router[j])
            h = _moe(h, hn, idx, gates, moe_w1[j].astype(BF16), moe_w3[j].astype(BF16),
                     moe_w2[j].astype(BF16))
    return h.reshape(batch, seq, D_MODEL)
```

```python
import functools
import math

import numpy as np
import jax
import jax.numpy as jnp
from jax import lax
from jax.experimental import pallas as pl
from jax.experimental.pallas import tpu as pltpu

F32 = jnp.float32
BF16 = jnp.bfloat16

D_MODEL = 2048
HEAD_DIM = 64
ATTN_WIDTH = 1024
RWKV_WIDTH = 512
SSM_WIDTH = 512
ATTN_HEADS = 16
ATTN_KV_HEADS = 4
ATTN_GROUP = 4
KV_WIDTH = 256
WINDOW = 128
BLOCK = 128
REL_BUCKETS = 32
REL_MAX_DISTANCE = 128
RWKV_HEADS = 8
RWKV_DECAY_LORA = 64
RWKV_A_LORA = 64
RWKV_GATE_LORA = 128
RWKV_COLS = 1792
RWKV_DECAY_SCALE = math.exp(-0.5)
RWKV_LN_EPS = HEAD_DIM * 1e-5
SSM_GROUP_CH = 16
SSM_GROUPS = 32
SSM_STATE = 64
SSM_STATES = SSM_GROUPS * SSM_STATE
Q_END = ATTN_WIDTH
K_END = Q_END + KV_WIDTH
V_END = K_END + KV_WIDTH
RWKV_END = V_END + RWKV_COLS
IN_COLS = RWKV_END + SSM_WIDTH
D_FF = 5632
N_EXPERTS = 8
TOP_K = 2
D_FF_EXPERT = 7168
NORM_EPS = 1e-6

LANES = 128
MXU_DIM = 256
VMEM_LIMIT = 56 * 1024 * 1024

ROW_TILE = 512
FFN_COL_TILE = 512
EXPERT_COL_TILE = 1024
SCAN_TILE = 256
S5_SEGMENTS = 8
S5_STRIP = 512
RWKV_CHUNK = 64
RWKV_QUAD = 4
GATHER_TILE = 512
GATHER_UNROLL = 8
ROUTER_LANES = 128


def _cparams(sem):
    return pltpu.CompilerParams(dimension_semantics=sem, vmem_limit_bytes=VMEM_LIMIT)


def _mm(a, b):
    return jnp.dot(a.astype(BF16), b.astype(BF16), preferred_element_type=F32)


def _mm_nt(a, b):
    return lax.dot_general(a.astype(BF16), b.astype(BF16), (((1,), (1,)), ((), ())),
                           preferred_element_type=F32)


def _mm_tn(a, b):
    return lax.dot_general(a.astype(BF16), b.astype(BF16), (((0,), (0,)), ((), ())),
                           preferred_element_type=F32)


def _split2(x):
    hi = x.astype(BF16)
    lo = (x - hi.astype(F32)).astype(BF16)
    return hi, lo


def _split3(x):
    h1 = x.astype(BF16)
    r1 = x - h1.astype(F32)
    h2 = r1.astype(BF16)
    h3 = (r1 - h2.astype(F32)).astype(BF16)
    return h1, h2, h3


def _seg_sum(x, ones_bd):
    hi, lo = _split2(x)
    return (jnp.dot(hi, ones_bd, preferred_element_type=F32)
            + jnp.dot(lo, ones_bd, preferred_element_type=F32))


def _const_spec(shape):
    nd = len(shape)
    return pl.BlockSpec(shape, lambda *_: (0,) * nd, pipeline_mode=pl.Buffered(1))


def _in_proj_kernel(x_ref, g_ref, w_ref, q_ref, kv_ref, rw_ref, ss_ref):
    x = x_ref[...]
    ms = jnp.mean(x * x, axis=-1, keepdims=True)
    hn = (x * lax.rsqrt(ms + NORM_EPS) * g_ref[...]).astype(BF16)
    q_ref[...] = jnp.dot(hn, w_ref[:, 0:Q_END], preferred_element_type=F32)
    kv_ref[...] = jnp.dot(hn, w_ref[:, Q_END:V_END], preferred_element_type=F32)
    rw_ref[...] = jnp.dot(hn, w_ref[:, V_END:RWKV_END], preferred_element_type=F32)
    ss_ref[...] = jnp.dot(hn, w_ref[:, RWKV_END:IN_COLS], preferred_element_type=F32)


def _in_proj(h, g, w):
    t = h.shape[0]
    tm = min(ROW_TILE, t)
    row = lambda w_: pl.BlockSpec((tm, w_), lambda i: (i, 0))
    return pl.pallas_call(
        _in_proj_kernel,
        grid=(t // tm,),
        in_specs=[row(D_MODEL), _const_spec((1, D_MODEL)), _const_spec((D_MODEL, IN_COLS))],
        out_specs=[row(ATTN_WIDTH), row(2 * KV_WIDTH), row(RWKV_COLS), row(SSM_WIDTH)],
        out_shape=[jax.ShapeDtypeStruct((t, ATTN_WIDTH), F32),
                   jax.ShapeDtypeStruct((t, 2 * KV_WIDTH), F32),
                   jax.ShapeDtypeStruct((t, RWKV_COLS), F32),
                   jax.ShapeDtypeStruct((t, SSM_WIDTH), F32)],
        compiler_params=_cparams(("arbitrary",)),
        name="in_proj",
    )(h, g.reshape(1, D_MODEL), w)


def _attn_kernel(sink_ref, q_ref, kvp_ref, kvc_ref, bias_ref, qg_ref, kg_ref, og_ref, e_ref, dup_ref,
                 o_ref, acc_ref):
    first = pl.program_id(1) == 0
    e = e_ref[...]
    kv = jnp.concatenate([kvp_ref[...], kvc_ref[...]], axis=0)
    rows = ATTN_GROUP * BLOCK

    def head_norm(t, g):
        ss = _seg_sum(t * t, e)
        return t * lax.rsqrt(ss * (1.0 / HEAD_DIM) + NORM_EPS) * g

    kn = head_norm(kv[:, :KV_WIDTH], kg_ref[...]).astype(BF16)
    v_dup = jnp.dot(kv[:, KV_WIDTH:].astype(BF16), dup_ref[...], preferred_element_type=F32).astype(BF16)
    ones = jnp.ones((2 * BLOCK, LANES), BF16)
    key_idx = lax.broadcasted_iota(jnp.int32, (rows, 2 * BLOCK), 1)
    no_prev = jnp.logical_and(first, key_idx < BLOCK)
    group = lax.broadcasted_iota(jnp.int32, (rows, 1), 0) // BLOCK
    low_half = lax.broadcasted_iota(jnp.int32, (1, LANES), 1) < HEAD_DIM
    scale = HEAD_DIM ** -0.5

    def scores(h):
        qn = (head_norm(q_ref[:, h * MXU_DIM:(h + 1) * MXU_DIM], qg_ref[...]) * scale).astype(BF16)
        qs = jnp.concatenate([qn[:, g * HEAD_DIM:(g + 1) * HEAD_DIM] for g in range(ATTN_GROUP)], axis=0)
        bias = bias_ref[h * ATTN_GROUP:(h + 1) * ATTN_GROUP].reshape(rows, 2 * BLOCK)
        return jnp.where(no_prev, -jnp.inf, _mm_nt(qs, kn[:, h * HEAD_DIM:(h + 1) * HEAD_DIM]) + bias)

    def finish(h, s):
        sink = jnp.full((rows, 1), sink_ref[h * ATTN_GROUP], F32)
        for g in range(1, ATTN_GROUP):
            sink = jnp.where(group == g, sink_ref[h * ATTN_GROUP + g], sink)
        m = jnp.maximum(jnp.max(s, axis=-1, keepdims=True), sink)
        p = jnp.exp(s - m).astype(BF16)
        rhs = jnp.concatenate([v_dup[:, h * LANES:(h + 1) * LANES], ones], axis=1)
        pv = jnp.dot(p, rhs, preferred_element_type=F32)
        out = pv[:, :LANES] / (pv[:, LANES:] + jnp.exp(sink - m))
        for pair in range(ATTN_GROUP // 2):
            even = out[(2 * pair) * BLOCK:(2 * pair + 1) * BLOCK]
            odd = out[(2 * pair + 1) * BLOCK:(2 * pair + 2) * BLOCK]
            col = (h * ATTN_GROUP + 2 * pair) * HEAD_DIM
            acc_ref[:, col:col + LANES] = jnp.where(low_half, even, odd)

    pending = scores(0)
    for h in range(ATTN_KV_HEADS):
        nxt = scores(h + 1) if h + 1 < ATTN_KV_HEADS else None
        finish(h, pending)
        pending = nxt
    o = acc_ref[...]
    ms = jnp.mean(o * o, axis=-1, keepdims=True)
    o_ref[...] = (o * lax.rsqrt(ms + NORM_EPS) * og_ref[...]).astype(BF16)


def _t5_bucket_table():
    qi = np.arange(BLOCK)[:, None]
    kj = np.arange(2 * BLOCK)[None, :]
    dist = BLOCK + qi - kj
    max_exact = REL_BUCKETS // 2
    d = np.maximum(dist, 0)
    large = max_exact + (np.log(np.maximum(d, 1).astype(np.float32) / max_exact)
                         / math.log(REL_MAX_DISTANCE / max_exact)
                         * (REL_BUCKETS - max_exact)).astype(np.int32)
    large = np.minimum(large, REL_BUCKETS - 1)
    bucket = np.where(d < max_exact, d, large)
    valid = (dist >= 0) & (dist < WINDOW)
    return bucket, valid


def _attention(q, kv, batch, seq, q_g, k_g, sinks, rel_bias, out_g):
    nb = seq // BLOCK
    bucket, valid = _t5_bucket_table()
    bias = jnp.transpose(rel_bias.astype(F32)[bucket], (2, 0, 1))
    bias = jnp.where(valid[None], bias, -jnp.inf)
    seg = np.arange(MXU_DIM) // HEAD_DIM
    ones_bd = jnp.asarray(seg[:, None] == seg[None, :], dtype=BF16)
    dup_src = (np.arange(2 * KV_WIDTH) // LANES) * HEAD_DIM + np.arange(2 * KV_WIDTH) % HEAD_DIM
    dup = jnp.asarray(np.arange(KV_WIDTH)[:, None] == dup_src[None, :], dtype=BF16)
    tile4 = lambda g: jnp.tile(g.astype(F32), ATTN_GROUP).reshape(1, MXU_DIM)
    grid_spec = pltpu.PrefetchScalarGridSpec(
        num_scalar_prefetch=1,
        grid=(batch, nb),
        in_specs=[
            pl.BlockSpec((BLOCK, ATTN_WIDTH), lambda b, n, s: (b * nb + n, 0)),
            pl.BlockSpec((BLOCK, 2 * KV_WIDTH), lambda b, n, s: (jnp.maximum(b * nb + n - 1, 0), 0)),
            pl.BlockSpec((BLOCK, 2 * KV_WIDTH), lambda b, n, s: (b * nb + n, 0)),
            pl.BlockSpec((ATTN_HEADS, BLOCK, 2 * BLOCK), lambda b, n, s: (0, 0, 0)),
            pl.BlockSpec((1, MXU_DIM), lambda b, n, s: (0, 0)),
            pl.BlockSpec((1, MXU_DIM), lambda b, n, s: (0, 0)),
            pl.BlockSpec((1, ATTN_WIDTH), lambda b, n, s: (0, 0)),
            pl.BlockSpec((MXU_DIM, MXU_DIM), lambda b, n, s: (0, 0)),
            pl.BlockSpec((KV_WIDTH, 2 * KV_WIDTH), lambda b, n, s: (0, 0)),
        ],
        out_specs=pl.BlockSpec((BLOCK, ATTN_WIDTH), lambda b, n, s: (b * nb + n, 0)),
        scratch_shapes=[pltpu.VMEM((BLOCK, ATTN_WIDTH), F32)],
    )
    return pl.pallas_call(
        _attn_kernel,
        grid_spec=grid_spec,
        out_shape=jax.ShapeDtypeStruct((batch * seq, ATTN_WIDTH), BF16),
        compiler_params=_cparams(("arbitrary", "arbitrary")),
        name="swa",
    )(sinks.astype(F32), q, kv, kv, bias, tile4(q_g), tile4(k_g), out_g.reshape(1, ATTN_WIDTH), ones_bd,
      dup)


def _rwkv_consts():
    c, g_heads = RWKV_CHUNK, RWKV_QUAD
    w = g_heads * HEAD_DIM
    row = lax.broadcasted_iota(jnp.int32, (c, g_heads * c), 0)
    col = lax.broadcasted_iota(jnp.int32, (c, g_heads * c), 1) % c
    strict = col < row
    eye = (row == col).astype(F32)
    diag2 = jnp.logical_and((row // 2) == (col // 2), strict)
    levels = []
    bs = 2
    while bs < c:
        levels.append(((row // (2 * bs)) == (col // (2 * bs))) & ((row // bs) % 2 == 1)
                      & ((col // bs) % 2 == 0))
        bs *= 2
    hrow = lax.broadcasted_iota(jnp.int32, (w, w), 0) // HEAD_DIM
    hcol = lax.broadcasted_iota(jnp.int32, (w, w), 1) // HEAD_DIM
    return strict, col <= row, eye, diag2, levels, hrow == hcol


def _rwkv_stack(x, bm):
    return jnp.tile(x.astype(BF16), (RWKV_QUAD, 1)) * bm


def _rwkv_prepare(items, bm, consts):
    strict, incl, eye, diag2, levels, _ = consts
    c = RWKV_CHUNK
    gc = RWKV_QUAD * c
    nt = (((1,), (1,)), ((), ()))
    out = []
    for (r, lw, cum, k, v, a, b) in items:
        last = cum[c - 1:c]
        p_inv = jnp.exp(-cum)
        ratio = jnp.exp(last - cum)
        out.append(dict(
            ar=jnp.concatenate([a * jnp.exp(cum - lw), r * jnp.exp(cum)], axis=0).astype(BF16),
            bks=jnp.concatenate([_rwkv_stack(b * p_inv, bm), _rwkv_stack(k * p_inv, bm)], axis=0),
            v=v, v_stack=_rwkv_stack(v, bm),
            uv_rhs=jnp.concatenate([b * ratio, k * ratio], axis=0).astype(BF16),
            p_last=jnp.exp(last)))
    bigs = [lax.dot_general(d["ar"], d.pop("bks"), nt, preferred_element_type=F32) for d in out]
    n_cats, t_invs = [], []
    for d, big in zip(out, bigs):
        n_cat = jnp.where(strict, big[:c, :gc], 0.0)
        n_cats.append(n_cat)
        t_invs.append(eye + jnp.where(diag2, n_cat, 0.0))
        d["r_b"] = jnp.where(incl, big[c:, :gc], 0.0).astype(BF16)
        d["akrk"] = jnp.concatenate([jnp.where(strict, big[:c, gc:], 0.0),
                                     jnp.where(incl, big[c:, gc:], 0.0)], axis=0).astype(BF16)
    for lm in levels:
        tns = [_mm(t, _rwkv_stack(jnp.where(lm, n, 0.0), bm)) for t, n in zip(t_invs, n_cats)]
        t_invs = [t + _mm(tn, _rwkv_stack(t, bm)) for t, tn in zip(t_invs, tns)]
    for d, t in zip(out, t_invs):
        d["t_inv"] = t.astype(BF16)
        d["wy"] = jnp.dot(d.pop("akrk"), d.pop("v_stack"), preferred_element_type=F32)
    return out


def _rwkv_apply(ds, hts, bm, consts):
    c = RWKV_CHUNK
    nt = (((1,), (1,)), ((), ()))
    ahs = [lax.dot_general(d["ar"], ht.astype(BF16), nt, preferred_element_type=F32)
           for d, ht in zip(ds, hts)]
    us = [_mm(d["t_inv"], _rwkv_stack(ah[:c] + d["wy"][:c], bm)) for d, ah in zip(ds, ahs)]
    upds = [_mm_tn(jnp.concatenate([u, d["v"]], axis=0), d["uv_rhs"]) for d, u in zip(ds, us)]
    ys = [ah[c:] + _mm(d["r_b"], _rwkv_stack(u, bm)) + d["wy"][c:] for d, ah, u in zip(ds, ahs, us)]
    new = [ht * d["p_last"] + jnp.where(consts[5], upd, 0.0) for d, ht, upd in zip(ds, hts, upds)]
    return ys, new


def _rwkv_kernel(z_ref, mu_ref, w2a_ref, g2_ref, w0_ref, a0_ref, kk_ref, ka_ref, rk_ref, lnw_ref,
                 lnb_ref, e_ref, tri_ref, bm_ref, o_ref, carry_ref, y_s, h_s):
    tb = z_ref.shape[0]
    wd = RWKV_WIDTH
    quad_w = RWKV_QUAD * HEAD_DIM
    n_quads = RWKV_HEADS // RWKV_QUAD
    c = RWKV_CHUNK

    @pl.when(pl.program_id(1) == 0)
    def _():
        carry_ref[...] = jnp.zeros_like(carry_ref)
        h_s[...] = jnp.zeros_like(h_s)

    z = z_ref[...]
    rows = lax.broadcasted_iota(jnp.int32, (tb, 1), 0)
    prev = jnp.where(rows == 0, carry_ref[...], pltpu.roll(z, 1, axis=0))
    carry_ref[...] = z[tb - 1:tb]
    zs = z + mu_ref[...] * (prev - z)
    r = zs[:, 0:wd]
    k = zs[:, wd:2 * wd]
    v = zs[:, 2 * wd:3 * wd]
    xg = zs[:, 3 * wd:3 * wd + RWKV_GATE_LORA]
    xwa = zs[:, 3 * wd + RWKV_GATE_LORA:]
    lane = lax.broadcasted_iota(jnp.int32, (1, RWKV_DECAY_LORA + RWKV_A_LORA), 1)
    lora_in = jnp.where(lane < RWKV_DECAY_LORA, jnp.tanh(xwa), xwa)
    lora = _mm(lora_in, w2a_ref[...])
    lw = -RWKV_DECAY_SCALE * jax.nn.sigmoid(w0_ref[...] + lora[:, :wd])
    iclr = jax.nn.sigmoid(a0_ref[...] + lora[:, wd:])
    gate = _mm(jax.nn.sigmoid(xg), g2_ref[...])
    kk = k * kk_ref[...]
    e = e_ref[...]
    seg = lambda x: jnp.concatenate(
        [_seg_sum(x[:, q * quad_w:(q + 1) * quad_w], e) for q in range(n_quads)], axis=1)
    kk = kk * lax.rsqrt(jnp.maximum(seg(kk * kk), 1e-12))
    k2 = k * (1.0 + (iclr - 1.0) * ka_ref[...])
    l1, l2, l3 = _split3(lw)
    tri = tri_ref[...]
    cum = (jnp.dot(tri, l1, preferred_element_type=F32) + jnp.dot(tri, l2, preferred_element_type=F32)
           + jnp.dot(tri, l3, preferred_element_type=F32))
    a = -kk
    b = kk * iclr

    consts = _rwkv_consts()
    bm = bm_ref[...]
    n_chunks = tb // c
    items = []
    for ci in range(n_chunks):
        for q in range(n_quads):
            sl = (slice(ci * c, (ci + 1) * c), slice(q * quad_w, (q + 1) * quad_w))
            items.append((r[sl], lw[sl], cum[sl], k2[sl], v[sl], a[sl], b[sl]))
    prepared = _rwkv_prepare(items, bm, consts)
    hts = [h_s[q] for q in range(n_quads)]
    for ci in range(n_chunks):
        ys, hts = _rwkv_apply(prepared[ci * n_quads:(ci + 1) * n_quads], hts, bm, consts)
        for q in range(n_quads):
            y_s[ci * c:(ci + 1) * c, q * quad_w:(q + 1) * quad_w] = ys[q]
    for q in range(n_quads):
        h_s[q] = hts[q]

    y = y_s[...]
    mean = seg(y) * (1.0 / HEAD_DIM)
    d = y - mean
    var = seg(d * d) * (1.0 / HEAD_DIM)
    yn = d * lax.rsqrt(var + RWKV_LN_EPS) * lnw_ref[...] + lnb_ref[...]
    bonus = seg(r * k2 * rk_ref[...]) * v
    o_ref[...] = ((yn + bonus) * gate).astype(BF16)


def _rwkv(z, batch, seq, mu, w0, w2, a0, a2, g2, k_k, k_a, r_k, ln_w, ln_b):
    tb = min(SCAN_TILE, seq)
    nb = seq // tb
    wd = RWKV_WIDTH
    quad_w = RWKV_QUAD * HEAD_DIM
    w2a = jnp.zeros((RWKV_DECAY_LORA + RWKV_A_LORA, 2 * wd), F32)
    w2a = w2a.at[:RWKV_DECAY_LORA, :wd].set(w2).at[RWKV_DECAY_LORA:, wd:].set(a2).astype(BF16)
    seg = np.arange(quad_w) // HEAD_DIM
    ones_bd = jnp.asarray(seg[:, None] == seg[None, :], dtype=BF16)
    t_idx = np.arange(tb)
    tri = jnp.asarray((t_idx[:, None] // RWKV_CHUNK == t_idx[None, :] // RWKV_CHUNK)
                      & (t_idx[None, :] <= t_idx[:, None]), dtype=BF16)
    stack_rows = np.arange(RWKV_QUAD * RWKV_CHUNK) // RWKV_CHUNK
    stack_mask = jnp.asarray(stack_rows[:, None] == seg[None, :], dtype=BF16)
    vec = lambda p: p.astype(F32).reshape(1, -1)
    return pl.pallas_call(
        _rwkv_kernel,
        grid=(batch, nb),
        in_specs=[pl.BlockSpec((tb, RWKV_COLS), lambda b, j: (b * nb + j, 0)),
                  _const_spec((1, RWKV_COLS)),
                  _const_spec((RWKV_DECAY_LORA + RWKV_A_LORA, 2 * wd)),
                  _const_spec((RWKV_GATE_LORA, wd))]
                 + [_const_spec((1, wd))] * 7
                 + [_const_spec((quad_w, quad_w)), _const_spec((tb, tb)),
                    _const_spec((RWKV_QUAD * RWKV_CHUNK, quad_w))],
        out_specs=pl.BlockSpec((tb, wd), lambda b, j: (b * nb + j, 0)),
        out_shape=jax.ShapeDtypeStruct((batch * seq, wd), BF16),
        scratch_shapes=[pltpu.VMEM((1, RWKV_COLS), F32), pltpu.VMEM((tb, wd), F32),
                        pltpu.VMEM((RWKV_HEADS // RWKV_QUAD, quad_w, quad_w), F32)],
        compiler_params=_cparams(("arbitrary", "arbitrary")),
        name="rwkv7",
    )(z, vec(mu), w2a, g2.astype(BF16), vec(w0), vec(a0), vec(k_k), vec(k_a), vec(r_k), vec(ln_w),
      vec(ln_b), ones_bd, tri, stack_mask)


def _cmul_add(ar, ai, xr, xi, br, bi):
    return ar * xr - ai * xi + br, ar * xi + ai * xr + bi


def _s5_kernel(u_ref, perm_ref, bbr_ref, bbi_ref, cr_ref, ci_ref, a1_ref, aseg_ref, apj_ref, ask_ref,
               d_ref, gw_ref, gb_ref, og_ref, o_ref, xr_s, xi_s, car_s, cai_s):
    tb = u_ref.shape[0]
    seg_len = tb // S5_SEGMENTS
    strip = S5_STRIP
    n_strips = SSM_STATES // strip
    n_groups = SSM_WIDTH // LANES

    @pl.when(pl.program_id(1) == 0)
    def _():
        car_s[...] = jnp.zeros_like(car_s)
        cai_s[...] = jnp.zeros_like(cai_s)

    perm = perm_ref[...]
    u1, u2, u3 = _split3(u_ref[...])
    u = (jnp.dot(perm, u1, preferred_element_type=F32) + jnp.dot(perm, u2, preferred_element_type=F32)
         + jnp.dot(perm, u3, preferred_element_type=F32))
    ub = u.astype(BF16)
    sw = SSM_STATES // n_groups
    for g in range(n_groups):
        ug = ub[:, g * LANES:(g + 1) * LANES]
        xr_s[:, g * sw:(g + 1) * sw] = jnp.dot(ug, bbr_ref[g * LANES:(g + 1) * LANES, g * sw:(g + 1) * sw],
                                               preferred_element_type=F32)
        xi_s[:, g * sw:(g + 1) * sw] = jnp.dot(ug, bbi_ref[g * LANES:(g + 1) * LANES, g * sw:(g + 1) * sw],
                                               preferred_element_type=F32)

    for s in range(n_strips):
        cols = slice(s * strip, (s + 1) * strip)
        ar = jnp.broadcast_to(a1_ref[0:1, cols], (S5_SEGMENTS, strip))
        ai = jnp.broadcast_to(a1_ref[1:2, cols], (S5_SEGMENTS, strip))

        def scan_body(j, x):
            rs = pl.ds(pl.multiple_of(j * S5_SEGMENTS, S5_SEGMENTS), S5_SEGMENTS)
            nr, ni = _cmul_add(ar, ai, x[0], x[1], xr_s[rs, cols], xi_s[rs, cols])
            xr_s[rs, cols] = nr
            xi_s[rs, cols] = ni
            return nr, ni

        fr, fi = lax.fori_loop(1, seg_len, scan_body,
                               (xr_s[0:S5_SEGMENTS, cols], xi_s[0:S5_SEGMENTS, cols]), unroll=2)
        seg_row = lax.broadcasted_iota(jnp.int32, (S5_SEGMENTS, strip), 0)
        ir, ii = fr, fi
        for lvl in range(int(math.log2(S5_SEGMENTS))):
            sh = 1 << lvl
            pr = jnp.where(seg_row >= sh, pltpu.roll(ir, sh, axis=0), 0.0)
            pi = jnp.where(seg_row >= sh, pltpu.roll(ii, sh, axis=0), 0.0)
            ir, ii = _cmul_add(aseg_ref[2 * lvl:2 * lvl + 1, cols], aseg_ref[2 * lvl + 1:2 * lvl + 2, cols],
                               pr, pi, ir, ii)
        c_in_r = car_s[:, cols]
        c_in_i = cai_s[:, cols]
        er = jnp.where(seg_row >= 1, pltpu.roll(ir, 1, axis=0), 0.0)
        ei = jnp.where(seg_row >= 1, pltpu.roll(ii, 1, axis=0), 0.0)
        cr_, ci_ = _cmul_add(ask_ref[0:S5_SEGMENTS, cols], ask_ref[S5_SEGMENTS:2 * S5_SEGMENTS, cols],
                             c_in_r, c_in_i, er, ei)
        last = S5_SEGMENTS - 1
        nr, ni = _cmul_add(aseg_ref[0:1, cols], aseg_ref[1:2, cols],
                           cr_[last:last + 1], ci_[last:last + 1], fr[last:last + 1], fi[last:last + 1])
        car_s[:, cols] = nr
        cai_s[:, cols] = ni

        def fix_body(j, carry):
            rs = pl.ds(pl.multiple_of(j * S5_SEGMENTS, S5_SEGMENTS), S5_SEGMENTS)
            pr = apj_ref[pl.ds(2 * j, 1), cols]
            pi = apj_ref[pl.ds(2 * j + 1, 1), cols]
            nr, ni = _cmul_add(pr, pi, cr_, ci_, xr_s[rs, cols], xi_s[rs, cols])
            xr_s[rs, cols] = nr
            xi_s[rs, cols] = ni
            return carry

        lax.fori_loop(0, seg_len, fix_body, 0, unroll=2)

    sg = SSM_STATES // n_groups
    ys = []
    for g in range(n_groups):
        xr = xr_s[:, g * sg:(g + 1) * sg].astype(BF16)
        xi = xi_s[:, g * sg:(g + 1) * sg].astype(BF16)
        ys.append(jnp.dot(xr, cr_ref[g * sg:(g + 1) * sg, g * LANES:(g + 1) * LANES], preferred_element_type=F32)
                  - jnp.dot(xi, ci_ref[g * sg:(g + 1) * sg, g * LANES:(g + 1) * LANES], preferred_element_type=F32))
    y = jnp.concatenate(ys, axis=1) + d_ref[...] * u
    zg = jax.nn.gelu(y)
    gl = _mm(zg, gw_ref[...]) + gb_ref[...]
    out = zg * jax.nn.sigmoid(gl)
    ms = jnp.mean(out * out, axis=-1, keepdims=True)
    out = (out * lax.rsqrt(ms + NORM_EPS) * og_ref[...]).astype(BF16)
    o_ref[...] = lax.dot_general(perm, out, (((0,), (0,)), ((), ())),
                                 preferred_element_type=F32).astype(BF16)


def _s5(u, batch, seq, lam_re, lam_im, b_re, b_im, c_re, c_im, d, log_dt, glu_w, glu_b, out_g):
    tb = min(SCAN_TILE, seq)
    nb = seq // tb
    seg_len = tb // S5_SEGMENTS
    f = lambda p: p.astype(F32)
    lr, li = f(lam_re), f(lam_im)
    dt = jnp.exp(f(log_dt))[:, None]
    mag = jnp.exp(lr * dt)
    ab_re = mag * jnp.cos(li * dt)
    ab_im = mag * jnp.sin(li * dt)
    den = lr * lr + li * li
    fr = ((ab_re - 1.0) * lr + ab_im * li) / den
    fi = (ab_im * lr - (ab_re - 1.0) * li) / den
    bb_re = fr[..., None] * f(b_re) - fi[..., None] * f(b_im)
    bb_im = fr[..., None] * f(b_im) + fi[..., None] * f(b_re)
    eye = jnp.eye(SSM_GROUPS, dtype=F32)
    in_bd = lambda t: jnp.einsum('gpm,gh->gmhp', t, eye).reshape(SSM_WIDTH, SSM_STATES).astype(BF16)
    out_bd = lambda t: jnp.einsum('gmp,gh->gphm', f(t), eye).reshape(SSM_STATES, SSM_WIDTH).astype(BF16)

    def a_pow(n):
        n = jnp.asarray(n, F32)[:, None, None]
        m = jnp.exp(n * (lr * dt))
        ang = n * (li * dt)
        both = jnp.stack([m * jnp.cos(ang), m * jnp.sin(ang)], axis=1)
        return both.reshape(-1, SSM_STATES)

    a1 = a_pow([1])
    aseg = a_pow([seg_len * (1 << l) for l in range(4)])
    apj = a_pow(np.arange(1, seg_len + 1))
    ask_both = a_pow(np.arange(S5_SEGMENTS) * seg_len).reshape(S5_SEGMENTS, 2, SSM_STATES)
    ask = jnp.concatenate([ask_both[:, 0], ask_both[:, 1]], axis=0)
    t_idx = np.arange(tb)
    src_time = (t_idx % S5_SEGMENTS) * seg_len + t_idx // S5_SEGMENTS
    perm = jnp.asarray(src_time[:, None] == t_idx[None, :], dtype=BF16)
    vec = lambda p: f(p).reshape(1, -1)
    buf = lambda: pltpu.VMEM((tb, SSM_STATES), F32)
    return pl.pallas_call(
        _s5_kernel,
        grid=(batch, nb),
        in_specs=[pl.BlockSpec((tb, SSM_WIDTH), lambda b, j: (b * nb + j, 0)),
                  _const_spec((tb, tb)),
                  _const_spec((SSM_WIDTH, SSM_STATES)), _const_spec((SSM_WIDTH, SSM_STATES)),
                  _const_spec((SSM_STATES, SSM_WIDTH)), _const_spec((SSM_STATES, SSM_WIDTH)),
                  _const_spec((2, SSM_STATES)), _const_spec((8, SSM_STATES)),
                  _const_spec((2 * seg_len, SSM_STATES)), _const_spec((2 * S5_SEGMENTS, SSM_STATES)),
                  _const_spec((1, SSM_WIDTH)), _const_spec((SSM_WIDTH, SSM_WIDTH)),
                  _const_spec((1, SSM_WIDTH)), _const_spec((1, SSM_WIDTH))],
        out_specs=pl.BlockSpec((tb, SSM_WIDTH), lambda b, j: (b * nb + j, 0)),
        out_shape=jax.ShapeDtypeStruct((batch * seq, SSM_WIDTH), BF16),
        scratch_shapes=[buf(), buf(),
                        pltpu.VMEM((1, SSM_STATES), F32), pltpu.VMEM((1, SSM_STATES), F32)],
        compiler_params=_cparams(("arbitrary", "arbitrary")),
        name="s5",
    )(u, perm, in_bd(bb_re), in_bd(bb_im), out_bd(c_re), out_bd(c_im), a1, aseg, apj, ask, vec(d),
      glu_w.astype(BF16), vec(glu_b), vec(out_g))


def _out_proj_kernel(h_ref, at_ref, rw_ref, ss_ref, w_ref, g_ref, *rest, with_router):
    if with_router:
        rt_ref, ho_ref, hn_ref, idx_ref, gate_ref = rest
    else:
        ho_ref, hn_ref = rest
    mix = (jnp.dot(at_ref[...], w_ref[0:ATTN_WIDTH, :], preferred_element_type=F32)
           + jnp.dot(rw_ref[...], w_ref[ATTN_WIDTH:ATTN_WIDTH + RWKV_WIDTH, :], preferred_element_type=F32)
           + jnp.dot(ss_ref[...], w_ref[ATTN_WIDTH + RWKV_WIDTH:, :], preferred_element_type=F32))
    h = h_ref[...] + mix
    ho_ref[...] = h
    ms = jnp.mean(h * h, axis=-1, keepdims=True)
    hn = h * lax.rsqrt(ms + NORM_EPS) * g_ref[...]
    if not with_router:
        hn_ref[...] = hn.astype(BF16)
        return
    hn_ref[...] = hn
    h_hi, h_lo = _split2(hn)
    r_hi, r_lo = _split2(rt_ref[...])
    logits = (jnp.dot(h_hi, r_hi, preferred_element_type=F32)
              + jnp.dot(h_lo, r_hi, preferred_element_type=F32)
              + jnp.dot(h_hi, r_lo, preferred_element_type=F32))
    lane_i = lax.broadcasted_iota(jnp.int32, logits.shape, 1)
    lane = lane_i.astype(F32)
    logits = jnp.where(lane_i < N_EXPERTS, logits, -jnp.inf)
    m1 = jnp.max(logits, axis=-1, keepdims=True)
    i1 = jnp.min(jnp.where(logits == m1, lane, float(ROUTER_LANES)), axis=-1, keepdims=True)
    rest_l = jnp.where(lane == i1, -jnp.inf, logits)
    m2 = jnp.max(rest_l, axis=-1, keepdims=True)
    i2 = jnp.min(jnp.where(rest_l == m2, lane, float(ROUTER_LANES)), axis=-1, keepdims=True)
    e2 = jnp.exp(m2 - m1)
    g1 = 1.0 / (1.0 + e2)
    g2 = e2 / (1.0 + e2)
    idx_ref[...] = jnp.where(lane_i == 0, i1, jnp.where(lane_i == 1, i2, 0.0)).astype(jnp.int32)
    gate_ref[...] = jnp.where(lane_i == 0, g1, jnp.where(lane_i == 1, g2, 0.0))


def _out_proj(h, attn, rw, ss, w_out, g2, router=None):
    t = h.shape[0]
    tm = min(ROW_TILE, t)
    row = lambda w_: pl.BlockSpec((tm, w_), lambda i: (i, 0))
    with_router = router is not None
    in_specs = [row(D_MODEL), row(ATTN_WIDTH), row(RWKV_WIDTH), row(SSM_WIDTH),
                _const_spec((D_MODEL, D_MODEL)), _const_spec((1, D_MODEL))]
    args = [h, attn, rw, ss, w_out, g2.reshape(1, D_MODEL)]
    out_specs = [row(D_MODEL), row(D_MODEL)]
    out_shape = [jax.ShapeDtypeStruct((t, D_MODEL), F32),
                 jax.ShapeDtypeStruct((t, D_MODEL), F32 if with_router else BF16)]
    if with_router:
        rt = jnp.zeros((D_MODEL, ROUTER_LANES), F32).at[:, :N_EXPERTS].set(router.astype(F32))
        in_specs.append(_const_spec((D_MODEL, ROUTER_LANES)))
        args.append(rt)
        out_specs += [row(ROUTER_LANES), row(ROUTER_LANES)]
        out_shape += [jax.ShapeDtypeStruct((t, ROUTER_LANES), jnp.int32),
                      jax.ShapeDtypeStruct((t, ROUTER_LANES), F32)]
    return pl.pallas_call(
        functools.partial(_out_proj_kernel, with_router=with_router),
        grid=(t // tm,),
        in_specs=in_specs, out_specs=out_specs, out_shape=out_shape,
        compiler_params=_cparams(("arbitrary",)),
        name="out_proj_router" if with_router else "out_proj",
    )(*args)


def _ffn_kernel(h_ref, hn_ref, w1_ref, w3_ref, w2_ref, o_ref):
    f = pl.program_id(1)

    @pl.when(f == 0)
    def _():
        o_ref[...] = h_ref[...]

    hn = hn_ref[...]
    a1 = jnp.dot(hn, w1_ref[...], preferred_element_type=F32)
    a3 = jnp.dot(hn, w3_ref[...], preferred_element_type=F32)
    act = (a1 * jax.nn.sigmoid(a1) * a3).astype(BF16)
    o_ref[...] += jnp.dot(act, w2_ref[...], preferred_element_type=F32)


def _ffn(h, hn, w1, w3, w2):
    t = h.shape[0]
    tm = min(ROW_TILE, t)
    d_ff = w1.shape[1]
    tf = FFN_COL_TILE
    return pl.pallas_call(
        _ffn_kernel,
        grid=(t // tm, d_ff // tf),
        in_specs=[pl.BlockSpec((tm, D_MODEL), lambda i, f: (i, 0)),
                  pl.BlockSpec((tm, D_MODEL), lambda i, f: (i, 0)),
                  pl.BlockSpec((D_MODEL, tf), lambda i, f: (0, f)),
                  pl.BlockSpec((D_MODEL, tf), lambda i, f: (0, f)),
                  pl.BlockSpec((tf, D_MODEL), lambda i, f: (f, 0))],
        out_specs=pl.BlockSpec((tm, D_MODEL), lambda i, f: (i, 0)),
        out_shape=jax.ShapeDtypeStruct((t, D_MODEL), F32),
        compiler_params=_cparams(("arbitrary", "arbitrary")),
        name="ffn",
    )(h, hn, w1, w3, w2)


def _dispatch_kernel(starts_ref, padded_ref, used_ref, pos_ref, src_ref, zero_ref, o_ref, sem, zsem):
    i = pl.program_id(0)
    tokens = pos_ref.shape[1] // TOP_K
    tm = zero_ref.shape[0]
    n_tiles = o_ref.shape[0] // tm

    def zero_tile(row0):
        return pltpu.make_async_copy(zero_ref, o_ref.at[pl.ds(pl.multiple_of(row0, tm), tm)], zsem)

    def tail_fill(e):
        return zero_tile(starts_ref[e] + padded_ref[e] - tm)

    @pl.when(i == 0)
    def _():
        for e in range(N_EXPERTS):
            pl.when(padded_ref[e] > 0)(lambda e=e: tail_fill(e).start())
        lax.fori_loop(used_ref[0], n_tiles, lambda j, c: (zero_tile(j * tm).start(), c)[1], 0)
        for e in range(N_EXPERTS):
            pl.when(padded_ref[e] > 0)(lambda e=e: tail_fill(e).wait())
        lax.fori_loop(used_ref[0], n_tiles, lambda j, c: (zero_tile(j * tm).wait(), c)[1], 0)

    def copy(r, k):
        return pltpu.make_async_copy(src_ref.at[pl.ds(r, 1)],
                                     o_ref.at[pl.ds(pos_ref[0, TOP_K * r + k], 1)], sem)

    def issue(r, c):
        for k in range(TOP_K):
            copy(r, k).start()
        return c

    def drain(r, c):
        for k in range(TOP_K):
            copy(r, k).wait()
        return c

    lax.fori_loop(0, tokens, issue, 0, unroll=GATHER_UNROLL)
    lax.fori_loop(0, tokens, drain, 0, unroll=GATHER_UNROLL)


def _dispatch(src, pos, starts, padded, n_used, n_rows, tm):
    t, width = src.shape
    tg = min(GATHER_TILE, t)
    grid_spec = pltpu.PrefetchScalarGridSpec(
        num_scalar_prefetch=3,
        grid=(t // tg,),
        in_specs=[pl.BlockSpec((None, 1, TOP_K * tg), lambda i, s, p, u: (i, 0, 0), memory_space=pltpu.SMEM),
                  pl.BlockSpec((tg, width), lambda i, s, p, u: (i, 0)),
                  pl.BlockSpec((tm, width), lambda i, s, p, u: (0, 0), pipeline_mode=pl.Buffered(1))],
        out_specs=pl.BlockSpec(memory_space=pl.ANY),
        scratch_shapes=[pltpu.SemaphoreType.DMA(()), pltpu.SemaphoreType.DMA(())],
    )
    return pl.pallas_call(
        _dispatch_kernel,
        grid_spec=grid_spec,
        out_shape=jax.ShapeDtypeStruct((n_rows, width), src.dtype),
        compiler_params=_cparams(("arbitrary",)),
        name="moe_dispatch",
    )(starts, padded, n_used, pos.reshape(t // tg, 1, TOP_K * tg), src, jnp.zeros((tm, width), src.dtype))


def _expert_kernel(te_ref, tv_ref, tx_ref, x_ref, w1_ref, w3_ref, w2_ref, o_ref, xb_ref):
    i = pl.program_id(0)
    f = pl.program_id(1)

    @pl.when(f == 0)
    def _():
        xb_ref[...] = x_ref[...].astype(BF16)
        o_ref[...] = jnp.zeros_like(o_ref)

    @pl.when(tv_ref[i] > 0)
    def _():
        xb = xb_ref[...]
        a1 = jnp.dot(xb, w1_ref[...], preferred_element_type=F32)
        a3 = jnp.dot(xb, w3_ref[...], preferred_element_type=F32)
        act = (a1 * jax.nn.sigmoid(a1) * a3).astype(BF16)
        o_ref[...] += jnp.dot(act, w2_ref[...], preferred_element_type=F32)


def _experts(xs, tile_expert, tile_valid, tile_src, w1, w3, w2, tm):
    n = xs.shape[0]
    d_ff = w1.shape[2]
    tf = EXPERT_COL_TILE
    nf = d_ff // tf
    fidx = lambda i, f, tv: jnp.where(tv[i] > 0, f, nf - 1)
    grid_spec = pltpu.PrefetchScalarGridSpec(
        num_scalar_prefetch=3,
        grid=(n // tm, nf),
        in_specs=[pl.BlockSpec((tm, D_MODEL), lambda i, f, te, tv, tx: (tx[i], 0)),
                  pl.BlockSpec((None, D_MODEL, tf), lambda i, f, te, tv, tx: (te[i], 0, fidx(i, f, tv))),
                  pl.BlockSpec((None, D_MODEL, tf), lambda i, f, te, tv, tx: (te[i], 0, fidx(i, f, tv))),
                  pl.BlockSpec((None, tf, D_MODEL), lambda i, f, te, tv, tx: (te[i], fidx(i, f, tv), 0))],
        out_specs=pl.BlockSpec((tm, D_MODEL), lambda i, f, te, tv, tx: (i, 0)),
        scratch_shapes=[pltpu.VMEM((tm, D_MODEL), BF16)],
    )
    return pl.pallas_call(
        _expert_kernel,
        grid_spec=grid_spec,
        out_shape=jax.ShapeDtypeStruct((n, D_MODEL), F32),
        compiler_params=_cparams(("arbitrary", "arbitrary")),
        name="moe_experts",
    )(tile_expert, tile_valid, tile_src, xs, w1, w3, w2)


def _combine_kernel(p0_ref, p1_ref, h_ref, gate_ref, ys_ref, o_ref, b0, b1, sem0, sem1):
    rows = o_ref.shape[0]

    def copies(r):
        return (pltpu.make_async_copy(ys_ref.at[pl.ds(p0_ref[0, r], 1)], b0.at[pl.ds(r, 1)], sem0),
                pltpu.make_async_copy(ys_ref.at[pl.ds(p1_ref[0, r], 1)], b1.at[pl.ds(r, 1)], sem1))

    def issue(r, c):
        c0, c1 = copies(r)
        c0.start()
        c1.start()
        return c

    def drain(r, c):
        c0, c1 = copies(r)
        c0.wait()
        c1.wait()
        return c

    lax.fori_loop(0, rows, issue, 0, unroll=GATHER_UNROLL)
    lax.fori_loop(0, rows, drain, 0, unroll=GATHER_UNROLL)
    gate = gate_ref[...]
    o_ref[...] = h_ref[...] + gate[:, 0:1] * b0[...] + gate[:, 1:2] * b1[...]


def _combine(h, gates, ys, pos0, pos1):
    t = h.shape[0]
    tg = min(GATHER_TILE, t)
    smem = pl.BlockSpec((None, 1, tg), lambda i: (i, 0, 0), memory_space=pltpu.SMEM)
    return pl.pallas_call(
        _combine_kernel,
        grid=(t // tg,),
        in_specs=[smem, smem,
                  pl.BlockSpec((tg, D_MODEL), lambda i: (i, 0)),
                  pl.BlockSpec((tg, ROUTER_LANES), lambda i: (i, 0)),
                  pl.BlockSpec(memory_space=pl.ANY)],
        out_specs=pl.BlockSpec((tg, D_MODEL), lambda i: (i, 0)),
        out_shape=jax.ShapeDtypeStruct((t, D_MODEL), F32),
        scratch_shapes=[pltpu.VMEM((tg, D_MODEL), F32), pltpu.VMEM((tg, D_MODEL), F32),
                        pltpu.SemaphoreType.DMA(()), pltpu.SemaphoreType.DMA(())],
        compiler_params=_cparams(("arbitrary",)),
        name="moe_combine",
    )(pos0.reshape(t // tg, 1, tg), pos1.reshape(t // tg, 1, tg), h, gates, ys)


def _moe(h, hn, idx, gates, w1, w3, w2):
    t = h.shape[0]
    tm = min(ROW_TILE, t)
    n_tiles = (t * TOP_K) // tm + N_EXPERTS
    n_rows = n_tiles * tm
    flat_e = idx[:, :TOP_K].reshape(-1)
    onehot = (flat_e[:, None] == jnp.arange(N_EXPERTS)[None, :]).astype(jnp.int32)
    rank = jnp.sum((jnp.cumsum(onehot, axis=0) - onehot) * onehot, axis=1)
    counts = jnp.sum(onehot, axis=0)
    padded = ((counts + tm - 1) // tm) * tm
    ends = jnp.cumsum(padded)
    starts = ends - padded
    pos = (starts[flat_e] + rank).astype(jnp.int32)
    tile_idx = jnp.arange(n_tiles, dtype=jnp.int32)
    tile_start = tile_idx * tm
    tile_valid = (tile_start < ends[-1]).astype(jnp.int32)
    tile_expert = jnp.minimum(jnp.sum((tile_start[:, None] >= ends[None, :]).astype(jnp.int32), axis=1),
                              N_EXPERTS - 1)
    last_tile = jnp.maximum(ends[-1] // tm - 1, 0).astype(jnp.int32)
    tile_expert = jnp.where(tile_valid > 0, tile_expert, tile_expert[last_tile]).astype(jnp.int32)
    tile_src = jnp.where(tile_valid > 0, tile_idx, last_tile)
    n_used = (ends[-1:] // tm).astype(jnp.int32)
    xs = _dispatch(hn, pos, starts.astype(jnp.int32), padded.astype(jnp.int32), n_used, n_rows, tm)
    ys = _experts(xs, tile_expert, tile_valid, tile_src, w1, w3, w2, tm)
    pos = pos.reshape(t, TOP_K)
    return _combine(h, gates, ys, pos[:, 0], pos[:, 1])


def _permute_in_cols(w):
    o = V_END
    wd = RWKV_WIDTH
    r = (o, o + wd)
    xw = (r[1], r[1] + RWKV_DECAY_LORA)
    k = (xw[1], xw[1] + wd)
    v = (k[1], k[1] + wd)
    xa = (v[1], v[1] + RWKV_A_LORA)
    xg = (xa[1], xa[1] + RWKV_GATE_LORA)
    order = [(0, o), r, k, v, xg, xw, xa, (RWKV_END, IN_COLS)]
    return jnp.concatenate([w[..., a:b] for a, b in order], axis=-1)


def kernel(x, norm1_g, w_in, q_norm_g, k_norm_g, attn_sinks, rel_bias, rwkv_mu, rwkv_w0, rwkv_w2, rwkv_a0, rwkv_a2, rwkv_g2, rwkv_k_k, rwkv_k_a, rwkv_r_k, rwkv_ln_w, rwkv_ln_b, ssm_lambda_re, ssm_lambda_im, ssm_b_re, ssm_b_im, ssm_c_re, ssm_c_im, ssm_d, ssm_log_dt, ssm_glu_w, ssm_glu_b, attn_out_g, ssm_out_g, w_out, norm2_g, ffn_w1, ffn_w3, ffn_w2, moe_router, moe_w1, moe_w3, moe_w2):
    batch, seq, _ = x.shape
    depth = w_in.shape[0]
    h = x.reshape(batch * seq, D_MODEL)
    for i in range(depth):
        w_in_i = _permute_in_cols(w_in[i]).astype(BF16)
        mu_i = _permute_in_cols(jnp.pad(rwkv_mu[i], (V_END, SSM_WIDTH)))[V_END:RWKV_END]
        q, kv, zr, zs = _in_proj(h, norm1_g[i], w_in_i)
        attn = _attention(q, kv, batch, seq, q_norm_g[i], k_norm_g[i], attn_sinks[i], rel_bias,
                          attn_out_g[i])
        rw = _rwkv(zr, batch, seq, mu_i, rwkv_w0[i], rwkv_w2[i], rwkv_a0[i], rwkv_a2[i], rwkv_g2[i],
                   rwkv_k_k[i], rwkv_k_a[i], rwkv_r_k[i], rwkv_ln_w[i], rwkv_ln_b[i])
        ss = _s5(zs, batch, seq, ssm_lambda_re[i], ssm_lambda_im[i], ssm_b_re[i], ssm_b_im[i],
                 ssm_c_re[i], ssm_c_im[i], ssm_d[i], ssm_log_dt[i], ssm_glu_w[i], ssm_glu_b[i],
                 ssm_out_g[i])
        j = i // 2
        if i % 2 == 0:
            h, hn = _out_proj(h, attn, rw, ss, w_out[i].astype(BF16), norm2_g[i])
            h = _ffn(h, hn, ffn_w1[j].astype(BF16), ffn_w3[j].astype(BF16), ffn_w2[j].astype(BF16))
        else:
            h, hn, idx, gates = _out_proj(h, attn, rw, ss, w_out[i].astype(BF16), norm2_g[i],
                                          router=moe_router[j])
            h = _moe(h, hn, idx, gates, moe_w1[j].astype(BF16), moe_w3[j].astype(BF16),
                     moe_w2[j].astype(BF16))
    return h.reshape(batch, seq, D_MODEL)
```

```python
import functools
import math

import numpy as np
import jax
import jax.numpy as jnp
from jax import lax
from jax.experimental import pallas as pl
from jax.experimental.pallas import tpu as pltpu

F32 = jnp.float32
BF16 = jnp.bfloat16

D_MODEL = 2048
HEAD_DIM = 64
ATTN_WIDTH = 1024
RWKV_WIDTH = 512
SSM_WIDTH = 512
ATTN_HEADS = 16
ATTN_KV_HEADS = 4
ATTN_GROUP = 4
KV_WIDTH = 256
WINDOW = 128
BLOCK = 128
REL_BUCKETS = 32
REL_MAX_DISTANCE = 128
RWKV_HEADS = 8
RWKV_DECAY_LORA = 64
RWKV_A_LORA = 64
RWKV_GATE_LORA = 128
RWKV_COLS = 1792
RWKV_DECAY_SCALE = math.exp(-0.5)
RWKV_LN_EPS = HEAD_DIM * 1e-5
SSM_GROUP_CH = 16
SSM_GROUPS = 32
SSM_STATE = 64
SSM_STATES = SSM_GROUPS * SSM_STATE
Q_END = ATTN_WIDTH
K_END = Q_END + KV_WIDTH
V_END = K_END + KV_WIDTH
RWKV_END = V_END + RWKV_COLS
IN_COLS = RWKV_END + SSM_WIDTH
D_FF = 5632
N_EXPERTS = 8
TOP_K = 2
D_FF_EXPERT = 7168
NORM_EPS = 1e-6

LANES = 128
MXU_DIM = 256
VMEM_LIMIT = 56 * 1024 * 1024

ROW_TILE = 512
FFN_COL_TILE = 512
EXPERT_COL_TILE = 1024
SCAN_TILE = 256
S5_SEGMENTS = 8
S5_STRIP = 512
RWKV_CHUNK = 64
RWKV_QUAD = 4
GATHER_TILE = 512
GATHER_UNROLL = 8
ROUTER_LANES = 128
ROUTER_SUBBLOCKS = 4


def _cparams(sem):
    return pltpu.CompilerParams(dimension_semantics=sem, vmem_limit_bytes=VMEM_LIMIT)


def _mm(a, b):
    return jnp.dot(a.astype(BF16), b.astype(BF16), preferred_element_type=F32)


def _mm_nt(a, b):
    return lax.dot_general(a.astype(BF16), b.astype(BF16), (((1,), (1,)), ((), ())),
                           preferred_element_type=F32)


def _mm_tn(a, b):
    return lax.dot_general(a.astype(BF16), b.astype(BF16), (((0,), (0,)), ((), ())),
                           preferred_element_type=F32)


def _split2(x):
    hi = x.astype(BF16)
    lo = (x - hi.astype(F32)).astype(BF16)
    return hi, lo


def _split3(x):
    h1 = x.astype(BF16)
    r1 = x - h1.astype(F32)
    h2 = r1.astype(BF16)
    h3 = (r1 - h2.astype(F32)).astype(BF16)
    return h1, h2, h3


def _seg_sum(x, ones_bd):
    hi, lo = _split2(x)
    return (jnp.dot(hi, ones_bd, preferred_element_type=F32)
            + jnp.dot(lo, ones_bd, preferred_element_type=F32))


def _const_spec(shape):
    nd = len(shape)
    return pl.BlockSpec(shape, lambda *_: (0,) * nd, pipeline_mode=pl.Buffered(1))


def _in_proj_kernel(x_ref, g_ref, w_ref, q_ref, kv_ref, rw_ref, ss_ref):
    x = x_ref[...]
    ms = jnp.mean(x * x, axis=-1, keepdims=True)
    hn = (x * lax.rsqrt(ms + NORM_EPS) * g_ref[...]).astype(BF16)
    q_ref[...] = jnp.dot(hn, w_ref[:, 0:Q_END], preferred_element_type=F32)
    kv_ref[...] = jnp.dot(hn, w_ref[:, Q_END:V_END], preferred_element_type=F32)
    rw_ref[...] = jnp.dot(hn, w_ref[:, V_END:RWKV_END], preferred_element_type=F32)
    ss_ref[...] = jnp.dot(hn, w_ref[:, RWKV_END:IN_COLS], preferred_element_type=F32)


def _in_proj(h, g, w):
    t = h.shape[0]
    tm = min(ROW_TILE, t)
    row = lambda w_: pl.BlockSpec((tm, w_), lambda i: (i, 0))
    return pl.pallas_call(
        _in_proj_kernel,
        grid=(t // tm,),
        in_specs=[row(D_MODEL), _const_spec((1, D_MODEL)), _const_spec((D_MODEL, IN_COLS))],
        out_specs=[row(ATTN_WIDTH), row(2 * KV_WIDTH), row(RWKV_COLS), row(SSM_WIDTH)],
        out_shape=[jax.ShapeDtypeStruct((t, ATTN_WIDTH), F32),
                   jax.ShapeDtypeStruct((t, 2 * KV_WIDTH), F32),
                   jax.ShapeDtypeStruct((t, RWKV_COLS), F32),
                   jax.ShapeDtypeStruct((t, SSM_WIDTH), F32)],
        compiler_params=_cparams(("arbitrary",)),
        name="in_proj",
    )(h, g.reshape(1, D_MODEL), w)


def _attn_kernel(sink_ref, q_ref, kvp_ref, kvc_ref, bias_ref, qg_ref, kg_ref, og_ref, e_ref, dup_ref,
                 o_ref, acc_ref):
    first = pl.program_id(1) == 0
    e = e_ref[...]
    kv = jnp.concatenate([kvp_ref[...], kvc_ref[...]], axis=0)
    rows = ATTN_GROUP * BLOCK

    def head_norm(t, g):
        ss = _seg_sum(t * t, e)
        return t * lax.rsqrt(ss * (1.0 / HEAD_DIM) + NORM_EPS) * g

    kn = head_norm(kv[:, :KV_WIDTH], kg_ref[...]).astype(BF16)
    v_dup = jnp.dot(kv[:, KV_WIDTH:].astype(BF16), dup_ref[...], preferred_element_type=F32).astype(BF16)
    ones = jnp.ones((2 * BLOCK, LANES), BF16)
    key_idx = lax.broadcasted_iota(jnp.int32, (rows, 2 * BLOCK), 1)
    no_prev = jnp.logical_and(first, key_idx < BLOCK)
    group = lax.broadcasted_iota(jnp.int32, (rows, 1), 0) // BLOCK
    low_half = lax.broadcasted_iota(jnp.int32, (1, LANES), 1) < HEAD_DIM
    scale = HEAD_DIM ** -0.5

    def scores(h):
        qn = (head_norm(q_ref[:, h * MXU_DIM:(h + 1) * MXU_DIM], qg_ref[...]) * scale).astype(BF16)
        qs = jnp.concatenate([qn[:, g * HEAD_DIM:(g + 1) * HEAD_DIM] for g in range(ATTN_GROUP)], axis=0)
        bias = bias_ref[h * ATTN_GROUP:(h + 1) * ATTN_GROUP].reshape(rows, 2 * BLOCK)
        return jnp.where(no_prev, -jnp.inf, _mm_nt(qs, kn[:, h * HEAD_DIM:(h + 1) * HEAD_DIM]) + bias)

    def finish(h, s):
        sink = jnp.full((rows, 1), sink_ref[h * ATTN_GROUP], F32)
        for g in range(1, ATTN_GROUP):
            sink = jnp.where(group == g, sink_ref[h * ATTN_GROUP + g], sink)
        m = jnp.maximum(jnp.max(s, axis=-1, keepdims=True), sink)
        p = jnp.exp(s - m).astype(BF16)
        rhs = jnp.concatenate([v_dup[:, h * LANES:(h + 1) * LANES], ones], axis=1)
        pv = jnp.dot(p, rhs, preferred_element_type=F32)
        out = pv[:, :LANES] / (pv[:, LANES:] + jnp.exp(sink - m))
        for pair in range(ATTN_GROUP // 2):
            even = out[(2 * pair) * BLOCK:(2 * pair + 1) * BLOCK]
            odd = out[(2 * pair + 1) * BLOCK:(2 * pair + 2) * BLOCK]
            col = (h * ATTN_GROUP + 2 * pair) * HEAD_DIM
            acc_ref[:, col:col + LANES] = jnp.where(low_half, even, odd)

    pending = scores(0)
    for h in range(ATTN_KV_HEADS):
        nxt = scores(h + 1) if h + 1 < ATTN_KV_HEADS else None
        finish(h, pending)
        pending = nxt
    o = acc_ref[...]
    ms = jnp.mean(o * o, axis=-1, keepdims=True)
    o_ref[...] = (o * lax.rsqrt(ms + NORM_EPS) * og_ref[...]).astype(BF16)


def _t5_bucket_table():
    qi = np.arange(BLOCK)[:, None]
    kj = np.arange(2 * BLOCK)[None, :]
    dist = BLOCK + qi - kj
    max_exact = REL_BUCKETS // 2
    d = np.maximum(dist, 0)
    large = max_exact + (np.log(np.maximum(d, 1).astype(np.float32) / max_exact)
                         / math.log(REL_MAX_DISTANCE / max_exact)
                         * (REL_BUCKETS - max_exact)).astype(np.int32)
    large = np.minimum(large, REL_BUCKETS - 1)
    bucket = np.where(d < max_exact, d, large)
    valid = (dist >= 0) & (dist < WINDOW)
    return bucket, valid


def _attention(q, kv, batch, seq, q_g, k_g, sinks, rel_bias, out_g):
    nb = seq // BLOCK
    bucket, valid = _t5_bucket_table()
    bias = jnp.transpose(rel_bias.astype(F32)[bucket], (2, 0, 1))
    bias = jnp.where(valid[None], bias, -jnp.inf)
    seg = np.arange(MXU_DIM) // HEAD_DIM
    ones_bd = jnp.asarray(seg[:, None] == seg[None, :], dtype=BF16)
    dup_src = (np.arange(2 * KV_WIDTH) // LANES) * HEAD_DIM + np.arange(2 * KV_WIDTH) % HEAD_DIM
    dup = jnp.asarray(np.arange(KV_WIDTH)[:, None] == dup_src[None, :], dtype=BF16)
    tile4 = lambda g: jnp.tile(g.astype(F32), ATTN_GROUP).reshape(1, MXU_DIM)
    grid_spec = pltpu.PrefetchScalarGridSpec(
        num_scalar_prefetch=1,
        grid=(batch, nb),
        in_specs=[
            pl.BlockSpec((BLOCK, ATTN_WIDTH), lambda b, n, s: (b * nb + n, 0)),
            pl.BlockSpec((BLOCK, 2 * KV_WIDTH), lambda b, n, s: (jnp.maximum(b * nb + n - 1, 0), 0)),
            pl.BlockSpec((BLOCK, 2 * KV_WIDTH), lambda b, n, s: (b * nb + n, 0)),
            pl.BlockSpec((ATTN_HEADS, BLOCK, 2 * BLOCK), lambda b, n, s: (0, 0, 0)),
            pl.BlockSpec((1, MXU_DIM), lambda b, n, s: (0, 0)),
            pl.BlockSpec((1, MXU_DIM), lambda b, n, s: (0, 0)),
            pl.BlockSpec((1, ATTN_WIDTH), lambda b, n, s: (0, 0)),
            pl.BlockSpec((MXU_DIM, MXU_DIM), lambda b, n, s: (0, 0)),
            pl.BlockSpec((KV_WIDTH, 2 * KV_WIDTH), lambda b, n, s: (0, 0)),
        ],
        out_specs=pl.BlockSpec((BLOCK, ATTN_WIDTH), lambda b, n, s: (b * nb + n, 0)),
        scratch_shapes=[pltpu.VMEM((BLOCK, ATTN_WIDTH), F32)],
    )
    return pl.pallas_call(
        _attn_kernel,
        grid_spec=grid_spec,
        out_shape=jax.ShapeDtypeStruct((batch * seq, ATTN_WIDTH), BF16),
        compiler_params=_cparams(("arbitrary", "arbitrary")),
        name="swa",
    )(sinks.astype(F32), q, kv, kv, bias, tile4(q_g), tile4(k_g), out_g.reshape(1, ATTN_WIDTH), ones_bd,
      dup)


def _rwkv_consts():
    c, g_heads = RWKV_CHUNK, RWKV_QUAD
    w = g_heads * HEAD_DIM
    row = lax.broadcasted_iota(jnp.int32, (c, g_heads * c), 0)
    col = lax.broadcasted_iota(jnp.int32, (c, g_heads * c), 1) % c
    strict = col < row
    eye = (row == col).astype(F32)
    diag2 = jnp.logical_and((row // 2) == (col // 2), strict)
    levels = []
    bs = 2
    while bs < c:
        levels.append(((row // (2 * bs)) == (col // (2 * bs))) & ((row // bs) % 2 == 1)
                      & ((col // bs) % 2 == 0))
        bs *= 2
    hrow = lax.broadcasted_iota(jnp.int32, (w, w), 0) // HEAD_DIM
    hcol = lax.broadcasted_iota(jnp.int32, (w, w), 1) // HEAD_DIM
    return strict, col <= row, eye, diag2, levels, hrow == hcol


def _rwkv_stack(x, bm):
    return jnp.tile(x.astype(BF16), (RWKV_QUAD, 1)) * bm


def _rwkv_prepare(items, bm, consts):
    strict, incl, eye, diag2, levels, _ = consts
    c = RWKV_CHUNK
    gc = RWKV_QUAD * c
    w = RWKV_QUAD * HEAD_DIM
    nt = (((1,), (1,)), ((), ()))

    def elementwise(d):
        r, lw, cum, k, v, a, b = d.pop("item")
        last = cum[c - 1:c]
        p_inv = jnp.exp(-cum)
        ratio = jnp.exp(last - cum)
        a_t = a * jnp.exp(cum - lw)
        d.update(
            ar=jnp.concatenate([a_t, r * jnp.exp(cum)], axis=0).astype(BF16),
            bks=jnp.concatenate([_rwkv_stack(b * p_inv, bm), _rwkv_stack(k * p_inv, bm)], axis=0),
            a_stack=_rwkv_stack(a_t, bm), v=v, v_stack=_rwkv_stack(v, bm),
            uv_t=jnp.concatenate([b * ratio, k * ratio], axis=0).T.astype(BF16),
            p_col=jnp.broadcast_to(jnp.exp(last), (8, w)).T[:, 0:1])

    def scores(d):
        big = lax.dot_general(d["ar"], d.pop("bks"), nt, preferred_element_type=F32)
        d["n_cat"] = jnp.where(strict, big[:c, :gc], 0.0)
        d["t"] = eye + jnp.where(diag2, d["n_cat"], 0.0)
        d["r_b"] = jnp.where(incl, big[c:, :gc], 0.0).astype(BF16)
        d["akrk"] = jnp.concatenate([jnp.where(strict, big[:c, gc:], 0.0),
                                     jnp.where(incl, big[c:, gc:], 0.0)], axis=0).astype(BF16)

    def level_first(lm):
        def stage(d):
            d["tn"] = _mm(d["t"], _rwkv_stack(jnp.where(lm, d["n_cat"], 0.0), bm))
        return stage

    def level_second(d):
        d["t"] = d["t"] + _mm(d.pop("tn"), _rwkv_stack(d["t"], bm))

    def values(d):
        d["wy"] = jnp.dot(d.pop("akrk"), d.pop("v_stack"), preferred_element_type=F32)

    def fold_inverse(d):
        wy = d.pop("wy")
        rhs = jnp.concatenate([d.pop("a_stack"), _rwkv_stack(wy[:c], bm)], axis=1)
        tw = jnp.dot(d.pop("t").astype(BF16), rhs, preferred_element_type=F32)
        d["lhs"] = jnp.concatenate([tw[:, :w].astype(BF16), d.pop("ar")[c:]], axis=0)
        d["uc"] = tw[:, w:]
        d["yc"] = wy[c:]
        d.pop("n_cat")

    stages = [elementwise, scores]
    for lm in levels:
        stages += [level_first(lm), level_second]
    stages += [values, fold_inverse]
    out = [dict(item=it) for it in items]
    for stage in stages:
        for d in out:
            stage(d)
    return out


def _rwkv_apply(ds, hss, bm, consts, between):
    c = RWKV_CHUNK
    ohs = [jnp.dot(d["lhs"], hs.astype(BF16), preferred_element_type=F32) for d, hs in zip(ds, hss)]
    between[0]()
    us = [oh[:c] + d["uc"] for d, oh in zip(ds, ohs)]
    upds = [jnp.dot(d["uv_t"], jnp.concatenate([u, d["v"]], axis=0).astype(BF16), preferred_element_type=F32)
            for d, u in zip(ds, us)]
    between[1]()
    ys = [oh[c:] + _mm(d["r_b"], _rwkv_stack(u, bm)) + d["yc"] for d, oh, u in zip(ds, ohs, us)]
    between[2]()
    new = [hs * d["p_col"] + jnp.where(consts[5], upd, 0.0) for d, hs, upd in zip(ds, hss, upds)]
    return ys, new


def _rwkv_kernel(z_ref, mu_ref, w2a_ref, g2_ref, w0_ref, a0_ref, kk_ref, ka_ref, rk_ref, lnw_ref,
                 lnb_ref, e_ref, tri_ref, bm_ref, o_ref, carry_ref, h_s):
    tb = z_ref.shape[0]
    wd = RWKV_WIDTH
    quad_w = RWKV_QUAD * HEAD_DIM
    n_quads = RWKV_HEADS // RWKV_QUAD
    c = RWKV_CHUNK

    @pl.when(pl.program_id(1) == 0)
    def _():
        carry_ref[...] = jnp.zeros_like(carry_ref)
        h_s[...] = jnp.zeros_like(h_s)

    z = z_ref[...]
    rows = lax.broadcasted_iota(jnp.int32, (tb, 1), 0)
    prev = jnp.where(rows == 0, carry_ref[...], pltpu.roll(z, 1, axis=0))
    carry_ref[...] = z[tb - 1:tb]
    zs = z + mu_ref[...] * (prev - z)
    r = zs[:, 0:wd]
    k = zs[:, wd:2 * wd]
    v = zs[:, 2 * wd:3 * wd]
    xg = zs[:, 3 * wd:3 * wd + RWKV_GATE_LORA]
    xwa = zs[:, 3 * wd + RWKV_GATE_LORA:]
    lane = lax.broadcasted_iota(jnp.int32, (1, RWKV_DECAY_LORA + RWKV_A_LORA), 1)
    lora_in = jnp.where(lane < RWKV_DECAY_LORA, jnp.tanh(xwa), xwa)
    lora = _mm(lora_in, w2a_ref[...])
    lw = -RWKV_DECAY_SCALE * jax.nn.sigmoid(w0_ref[...] + lora[:, :wd])
    iclr = jax.nn.sigmoid(a0_ref[...] + lora[:, wd:])
    gate = _mm(jax.nn.sigmoid(xg), g2_ref[...])
    kk = k * kk_ref[...]
    e = e_ref[...]
    seg = lambda x: jnp.concatenate(
        [_seg_sum(x[:, q * quad_w:(q + 1) * quad_w], e) for q in range(n_quads)], axis=1)
    kk = kk * lax.rsqrt(jnp.maximum(seg(kk * kk), 1e-12))
    k2 = k * (1.0 + (iclr - 1.0) * ka_ref[...])
    l1, l2, l3 = _split3(lw)
    tri = tri_ref[...]
    cum = (jnp.dot(tri, l1, preferred_element_type=F32) + jnp.dot(tri, l2, preferred_element_type=F32)
           + jnp.dot(tri, l3, preferred_element_type=F32))
    a = -kk
    b = kk * iclr

    consts = _rwkv_consts()
    bm = bm_ref[...]
    n_chunks = tb // c
    items = []
    for ci in range(n_chunks):
        for q in range(n_quads):
            sl = (slice(ci * c, (ci + 1) * c), slice(q * quad_w, (q + 1) * quad_w))
            items.append((r[sl], lw[sl], cum[sl], k2[sl], v[sl], a[sl], b[sl]))
    bonus = seg(r * k2 * rk_ref[...]) * v
    prepared = _rwkv_prepare(items, bm, consts)
    lnw = lnw_ref[...]
    lnb = lnb_ref[...]

    def group_norm_stages(p):
        if p is None:
            return (lambda: None,) * 3

        def mean():
            p["mean"] = seg(p["y"]) * (1.0 / HEAD_DIM)

        def var():
            p["d"] = p["y"] - p["mean"]
            p["var"] = seg(p["d"] * p["d"]) * (1.0 / HEAD_DIM)

        def store():
            rows = p["rows"]
            yn = p["d"] * lax.rsqrt(p["var"] + RWKV_LN_EPS) * lnw + lnb
            o_ref[rows, :] = ((yn + bonus[rows]) * gate[rows]).astype(BF16)

        return mean, var, store

    hts = [h_s[q] for q in range(n_quads)]
    done = None
    for ci in range(n_chunks):
        ys, hts = _rwkv_apply(prepared[ci * n_quads:(ci + 1) * n_quads], hts, bm, consts,
                              group_norm_stages(done))
        done = dict(y=jnp.concatenate(ys, axis=1), rows=slice(ci * c, (ci + 1) * c))
    for q in range(n_quads):
        h_s[q] = hts[q]
    for stage in group_norm_stages(done):
        stage()


def _rwkv(z, batch, seq, mu, w0, w2, a0, a2, g2, k_k, k_a, r_k, ln_w, ln_b):
    tb = min(SCAN_TILE, seq)
    nb = seq // tb
    wd = RWKV_WIDTH
    quad_w = RWKV_QUAD * HEAD_DIM
    w2a = jnp.zeros((RWKV_DECAY_LORA + RWKV_A_LORA, 2 * wd), F32)
    w2a = w2a.at[:RWKV_DECAY_LORA, :wd].set(w2).at[RWKV_DECAY_LORA:, wd:].set(a2).astype(BF16)
    seg = np.arange(quad_w) // HEAD_DIM
    ones_bd = jnp.asarray(seg[:, None] == seg[None, :], dtype=BF16)
    t_idx = np.arange(tb)
    tri = jnp.asarray((t_idx[:, None] // RWKV_CHUNK == t_idx[None, :] // RWKV_CHUNK)
                      & (t_idx[None, :] <= t_idx[:, None]), dtype=BF16)
    stack_rows = np.arange(RWKV_QUAD * RWKV_CHUNK) // RWKV_CHUNK
    stack_mask = jnp.asarray(stack_rows[:, None] == seg[None, :], dtype=BF16)
    vec = lambda p: p.astype(F32).reshape(1, -1)
    return pl.pallas_call(
        _rwkv_kernel,
        grid=(batch, nb),
        in_specs=[pl.BlockSpec((tb, RWKV_COLS), lambda b, j: (b * nb + j, 0)),
                  _const_spec((1, RWKV_COLS)),
                  _const_spec((RWKV_DECAY_LORA + RWKV_A_LORA, 2 * wd)),
                  _const_spec((RWKV_GATE_LORA, wd))]
                 + [_const_spec((1, wd))] * 7
                 + [_const_spec((quad_w, quad_w)), _const_spec((tb, tb)),
                    _const_spec((RWKV_QUAD * RWKV_CHUNK, quad_w))],
        out_specs=pl.BlockSpec((tb, wd), lambda b, j: (b * nb + j, 0)),
        out_shape=jax.ShapeDtypeStruct((batch * seq, wd), BF16),
        scratch_shapes=[pltpu.VMEM((1, RWKV_COLS), F32),
                        pltpu.VMEM((RWKV_HEADS // RWKV_QUAD, quad_w, quad_w), F32)],
        compiler_params=_cparams(("arbitrary", "arbitrary")),
        name="rwkv7",
    )(z, vec(mu), w2a, g2.astype(BF16), vec(w0), vec(a0), vec(k_k), vec(k_a), vec(r_k), vec(ln_w),
      vec(ln_b), ones_bd, tri, stack_mask)


def _cmul_add(ar, ai, xr, xi, br, bi):
    return ar * xr - ai * xi + br, ar * xi + ai * xr + bi


def _s5_kernel(u_ref, perm_ref, bbr_ref, bbi_ref, cr_ref, ci_ref, a1_ref, aseg_ref, apj_ref, ask_ref,
               d_ref, gw_ref, gb_ref, og_ref, o_ref, xr_s, xi_s, car_s, cai_s):
    tb = u_ref.shape[0]
    seg_len = tb // S5_SEGMENTS
    strip = S5_STRIP
    n_strips = SSM_STATES // strip
    n_groups = SSM_WIDTH // LANES

    @pl.when(pl.program_id(1) == 0)
    def _():
        car_s[...] = jnp.zeros_like(car_s)
        cai_s[...] = jnp.zeros_like(cai_s)

    perm = perm_ref[...]
    u1, u2, u3 = _split3(u_ref[...])
    u = (jnp.dot(perm, u1, preferred_element_type=F32) + jnp.dot(perm, u2, preferred_element_type=F32)
         + jnp.dot(perm, u3, preferred_element_type=F32))
    ub = u.astype(BF16)
    sw = SSM_STATES // n_groups
    for g in range(n_groups):
        ug = ub[:, g * LANES:(g + 1) * LANES]
        xr_s[:, g * sw:(g + 1) * sw] = jnp.dot(ug, bbr_ref[g * LANES:(g + 1) * LANES, g * sw:(g + 1) * sw],
                                               preferred_element_type=F32)
        xi_s[:, g * sw:(g + 1) * sw] = jnp.dot(ug, bbi_ref[g * LANES:(g + 1) * LANES, g * sw:(g + 1) * sw],
                                               preferred_element_type=F32)

    for s in range(n_strips):
        cols = slice(s * strip, (s + 1) * strip)
        ar = jnp.broadcast_to(a1_ref[0:1, cols], (S5_SEGMENTS, strip))
        ai = jnp.broadcast_to(a1_ref[1:2, cols], (S5_SEGMENTS, strip))

        def scan_body(j, x):
            rs = pl.ds(pl.multiple_of(j * S5_SEGMENTS, S5_SEGMENTS), S5_SEGMENTS)
            nr, ni = _cmul_add(ar, ai, x[0], x[1], xr_s[rs, cols], xi_s[rs, cols])
            xr_s[rs, cols] = nr
            xi_s[rs, cols] = ni
            return nr, ni

        fr, fi = lax.fori_loop(1, seg_len, scan_body,
                               (xr_s[0:S5_SEGMENTS, cols], xi_s[0:S5_SEGMENTS, cols]), unroll=2)
        seg_row = lax.broadcasted_iota(jnp.int32, (S5_SEGMENTS, strip), 0)
        ir, ii = fr, fi
        for lvl in range(int(math.log2(S5_SEGMENTS))):
            sh = 1 << lvl
            pr = jnp.where(seg_row >= sh, pltpu.roll(ir, sh, axis=0), 0.0)
            pi = jnp.where(seg_row >= sh, pltpu.roll(ii, sh, axis=0), 0.0)
            ir, ii = _cmul_add(aseg_ref[2 * lvl:2 * lvl + 1, cols], aseg_ref[2 * lvl + 1:2 * lvl + 2, cols],
                               pr, pi, ir, ii)
        c_in_r = car_s[:, cols]
        c_in_i = cai_s[:, cols]
        er = jnp.where(seg_row >= 1, pltpu.roll(ir, 1, axis=0), 0.0)
        ei = jnp.where(seg_row >= 1, pltpu.roll(ii, 1, axis=0), 0.0)
        cr_, ci_ = _cmul_add(ask_ref[0:S5_SEGMENTS, cols], ask_ref[S5_SEGMENTS:2 * S5_SEGMENTS, cols],
                             c_in_r, c_in_i, er, ei)
        last = S5_SEGMENTS - 1
        nr, ni = _cmul_add(aseg_ref[0:1, cols], aseg_ref[1:2, cols],
                           cr_[last:last + 1], ci_[last:last + 1], fr[last:last + 1], fi[last:last + 1])
        car_s[:, cols] = nr
        cai_s[:, cols] = ni

        def fix_body(j, carry):
            rs = pl.ds(pl.multiple_of(j * S5_SEGMENTS, S5_SEGMENTS), S5_SEGMENTS)
            pr = apj_ref[pl.ds(2 * j, 1), cols]
            pi = apj_ref[pl.ds(2 * j + 1, 1), cols]
            nr, ni = _cmul_add(pr, pi, cr_, ci_, xr_s[rs, cols], xi_s[rs, cols])
            xr_s[rs, cols] = nr
            xi_s[rs, cols] = ni
            return carry

        lax.fori_loop(0, seg_len, fix_body, 0, unroll=2)

    sg = SSM_STATES // n_groups
    ys = []
    for g in range(n_groups):
        xr = xr_s[:, g * sg:(g + 1) * sg].astype(BF16)
        xi = xi_s[:, g * sg:(g + 1) * sg].astype(BF16)
        ys.append(jnp.dot(xr, cr_ref[g * sg:(g + 1) * sg, g * LANES:(g + 1) * LANES], preferred_element_type=F32)
                  - jnp.dot(xi, ci_ref[g * sg:(g + 1) * sg, g * LANES:(g + 1) * LANES], preferred_element_type=F32))
    y = jnp.concatenate(ys, axis=1) + d_ref[...] * u
    zg = jax.nn.gelu(y)
    gl = _mm(zg, gw_ref[...]) + gb_ref[...]
    out = zg * jax.nn.sigmoid(gl)
    ms = jnp.mean(out * out, axis=-1, keepdims=True)
    out = (out * lax.rsqrt(ms + NORM_EPS) * og_ref[...]).astype(BF16)
    o_ref[...] = lax.dot_general(perm, out, (((0,), (0,)), ((), ())),
                                 preferred_element_type=F32).astype(BF16)


def _s5(u, batch, seq, lam_re, lam_im, b_re, b_im, c_re, c_im, d, log_dt, glu_w, glu_b, out_g):
    tb = min(SCAN_TILE, seq)
    nb = seq // tb
    seg_len = tb // S5_SEGMENTS
    f = lambda p: p.astype(F32)
    lr, li = f(lam_re), f(lam_im)
    dt = jnp.exp(f(log_dt))[:, None]
    mag = jnp.exp(lr * dt)
    ab_re = mag * jnp.cos(li * dt)
    ab_im = mag * jnp.sin(li * dt)
    den = lr * lr + li * li
    fr = ((ab_re - 1.0) * lr + ab_im * li) / den
    fi = (ab_im * lr - (ab_re - 1.0) * li) / den
    bb_re = fr[..., None] * f(b_re) - fi[..., None] * f(b_im)
    bb_im = fr[..., None] * f(b_im) + fi[..., None] * f(b_re)
    eye = jnp.eye(SSM_GROUPS, dtype=F32)
    in_bd = lambda t: jnp.einsum('gpm,gh->gmhp', t, eye).reshape(SSM_WIDTH, SSM_STATES).astype(BF16)
    out_bd = lambda t: jnp.einsum('gmp,gh->gphm', f(t), eye).reshape(SSM_STATES, SSM_WIDTH).astype(BF16)

    def a_pow(n):
        n = jnp.asarray(n, F32)[:, None, None]
        m = jnp.exp(n * (lr * dt))
        ang = n * (li * dt)
        both = jnp.stack([m * jnp.cos(ang), m * jnp.sin(ang)], axis=1)
        return both.reshape(-1, SSM_STATES)

    a1 = a_pow([1])
    aseg = a_pow([seg_len * (1 << l) for l in range(4)])
    apj = a_pow(np.arange(1, seg_len + 1))
    ask_both = a_pow(np.arange(S5_SEGMENTS) * seg_len).reshape(S5_SEGMENTS, 2, SSM_STATES)
    ask = jnp.concatenate([ask_both[:, 0], ask_both[:, 1]], axis=0)
    t_idx = np.arange(tb)
    src_time = (t_idx % S5_SEGMENTS) * seg_len + t_idx // S5_SEGMENTS
    perm = jnp.asarray(src_time[:, None] == t_idx[None, :], dtype=BF16)
    vec = lambda p: f(p).reshape(1, -1)
    buf = lambda: pltpu.VMEM((tb, SSM_STATES), F32)
    return pl.pallas_call(
        _s5_kernel,
        grid=(batch, nb),
        in_specs=[pl.BlockSpec((tb, SSM_WIDTH), lambda b, j: (b * nb + j, 0)),
                  _const_spec((tb, tb)),
                  _const_spec((SSM_WIDTH, SSM_STATES)), _const_spec((SSM_WIDTH, SSM_STATES)),
                  _const_spec((SSM_STATES, SSM_WIDTH)), _const_spec((SSM_STATES, SSM_WIDTH)),
                  _const_spec((2, SSM_STATES)), _const_spec((8, SSM_STATES)),
                  _const_spec((2 * seg_len, SSM_STATES)), _const_spec((2 * S5_SEGMENTS, SSM_STATES)),
                  _const_spec((1, SSM_WIDTH)), _const_spec((SSM_WIDTH, SSM_WIDTH)),
                  _const_spec((1, SSM_WIDTH)), _const_spec((1, SSM_WIDTH))],
        out_specs=pl.BlockSpec((tb, SSM_WIDTH), lambda b, j: (b * nb + j, 0)),
        out_shape=jax.ShapeDtypeStruct((batch * seq, SSM_WIDTH), BF16),
        scratch_shapes=[buf(), buf(),
                        pltpu.VMEM((1, SSM_STATES), F32), pltpu.VMEM((1, SSM_STATES), F32)],
        compiler_params=_cparams(("arbitrary", "arbitrary")),
        name="s5",
    )(u, perm, in_bd(bb_re), in_bd(bb_im), out_bd(c_re), out_bd(c_im), a1, aseg, apj, ask, vec(d),
      glu_w.astype(BF16), vec(glu_b), vec(out_g))


def _out_proj_kernel(h_ref, at_ref, rw_ref, ss_ref, w_ref, g_ref, *rest, with_router):
    if with_router:
        rt_ref, ho_ref, hn_ref, idx_ref, gate_ref = rest
    else:
        ho_ref, hn_ref = rest

    def project(rows):
        mix = (jnp.dot(at_ref[rows, :], w_ref[0:ATTN_WIDTH, :], preferred_element_type=F32)
               + jnp.dot(rw_ref[rows, :], w_ref[ATTN_WIDTH:ATTN_WIDTH + RWKV_WIDTH, :],
                         preferred_element_type=F32)
               + jnp.dot(ss_ref[rows, :], w_ref[ATTN_WIDTH + RWKV_WIDTH:, :], preferred_element_type=F32))
        h = h_ref[rows, :] + mix
        ho_ref[rows, :] = h
        ms = jnp.mean(h * h, axis=-1, keepdims=True)
        hn = h * lax.rsqrt(ms + NORM_EPS) * g_ref[...]
        hn_ref[rows, :] = hn.astype(hn_ref.dtype)
        return hn

    def route(rows, hn):
        h_hi, h_lo = _split2(hn)
        r_hi, r_lo = _split2(rt_ref[...])
        logits = (jnp.dot(h_hi, r_hi, preferred_element_type=F32)
                  + jnp.dot(h_lo, r_hi, preferred_element_type=F32)
                  + jnp.dot(h_hi, r_lo, preferred_element_type=F32))
        lane_i = lax.broadcasted_iota(jnp.int32, logits.shape, 1)
        lane = lane_i.astype(F32)
        logits = jnp.where(lane_i < N_EXPERTS, logits, -jnp.inf)
        m1 = jnp.max(logits, axis=-1, keepdims=True)
        i1 = jnp.min(jnp.where(logits == m1, lane, float(ROUTER_LANES)), axis=-1, keepdims=True)
        rest_l = jnp.where(lane == i1, -jnp.inf, logits)
        m2 = jnp.max(rest_l, axis=-1, keepdims=True)
        i2 = jnp.min(jnp.where(rest_l == m2, lane, float(ROUTER_LANES)), axis=-1, keepdims=True)
        e2 = jnp.exp(m2 - m1)
        g1 = 1.0 / (1.0 + e2)
        g2 = e2 / (1.0 + e2)
        idx_ref[rows, :] = jnp.where(lane_i == 0, i1, jnp.where(lane_i == 1, i2, 0.0)).astype(jnp.int32)
        gate_ref[rows, :] = jnp.where(lane_i == 0, g1, jnp.where(lane_i == 1, g2, 0.0))

    if not with_router:
        project(slice(None))
        return
    sub = h_ref.shape[0] // ROUTER_SUBBLOCKS
    pending = None
    for j in range(ROUTER_SUBBLOCKS):
        rows = slice(j * sub, (j + 1) * sub)
        hn = project(rows)
        if pending is not None:
            route(*pending)
        pending = (rows, hn)
    route(*pending)


def _out_proj(h, attn, rw, ss, w_out, g2, router=None):
    t = h.shape[0]
    tm = min(ROW_TILE, t)
    row = lambda w_: pl.BlockSpec((tm, w_), lambda i: (i, 0))
    with_router = router is not None
    in_specs = [row(D_MODEL), row(ATTN_WIDTH), row(RWKV_WIDTH), row(SSM_WIDTH),
                _const_spec((D_MODEL, D_MODEL)), _const_spec((1, D_MODEL))]
    args = [h, attn, rw, ss, w_out, g2.reshape(1, D_MODEL)]
    out_specs = [row(D_MODEL), row(D_MODEL)]
    out_shape = [jax.ShapeDtypeStruct((t, D_MODEL), F32),
                 jax.ShapeDtypeStruct((t, D_MODEL), F32 if with_router else BF16)]
    if with_router:
        rt = jnp.zeros((D_MODEL, ROUTER_LANES), F32).at[:, :N_EXPERTS].set(router.astype(F32))
        in_specs.append(_const_spec((D_MODEL, ROUTER_LANES)))
        args.append(rt)
        out_specs += [row(ROUTER_LANES), row(ROUTER_LANES)]
        out_shape += [jax.ShapeDtypeStruct((t, ROUTER_LANES), jnp.int32),
                      jax.ShapeDtypeStruct((t, ROUTER_LANES), F32)]
    return pl.pallas_call(
        functools.partial(_out_proj_kernel, with_router=with_router),
        grid=(t // tm,),
        in_specs=in_specs, out_specs=out_specs, out_shape=out_shape,
        compiler_params=_cparams(("arbitrary",)),
        name="out_proj_router" if with_router else "out_proj",
    )(*args)


def _ffn_kernel(h_ref, hn_ref, w1_ref, w3_ref, w2_ref, o_ref):
    f = pl.program_id(1)

    @pl.when(f == 0)
    def _():
        o_ref[...] = h_ref[...]

    hn = hn_ref[...]
    a1 = jnp.dot(hn, w1_ref[...], preferred_element_type=F32)
    a3 = jnp.dot(hn, w3_ref[...], preferred_element_type=F32)
    act = (a1 * jax.nn.sigmoid(a1) * a3).astype(BF16)
    o_ref[...] += jnp.dot(act, w2_ref[...], preferred_element_type=F32)


def _ffn(h, hn, w1, w3, w2):
    t = h.shape[0]
    tm = min(ROW_TILE, t)
    d_ff = w1.shape[1]
    tf = FFN_COL_TILE
    return pl.pallas_call(
        _ffn_kernel,
        grid=(t // tm, d_ff // tf),
        in_specs=[pl.BlockSpec((tm, D_MODEL), lambda i, f: (i, 0)),
                  pl.BlockSpec((tm, D_MODEL), lambda i, f: (i, 0)),
                  pl.BlockSpec((D_MODEL, tf), lambda i, f: (0, f)),
                  pl.BlockSpec((D_MODEL, tf), lambda i, f: (0, f)),
                  pl.BlockSpec((tf, D_MODEL), lambda i, f: (f, 0))],
        out_specs=pl.BlockSpec((tm, D_MODEL), lambda i, f: (i, 0)),
        out_shape=jax.ShapeDtypeStruct((t, D_MODEL), F32),
        compiler_params=_cparams(("arbitrary", "arbitrary")),
        name="ffn",
    )(h, hn, w1, w3, w2)


def _dispatch_kernel(starts_ref, padded_ref, used_ref, pos_ref, src_ref, zero_ref, o_ref, sem, zsem):
    i = pl.program_id(0)
    tokens = pos_ref.shape[1] // TOP_K
    tm = zero_ref.shape[0]
    n_tiles = o_ref.shape[0] // tm

    def zero_tile(row0):
        return pltpu.make_async_copy(zero_ref, o_ref.at[pl.ds(pl.multiple_of(row0, tm), tm)], zsem)

    def tail_fill(e):
        return zero_tile(starts_ref[e] + padded_ref[e] - tm)

    @pl.when(i == 0)
    def _():
        for e in range(N_EXPERTS):
            pl.when(padded_ref[e] > 0)(lambda e=e: tail_fill(e).start())
        lax.fori_loop(used_ref[0], n_tiles, lambda j, c: (zero_tile(j * tm).start(), c)[1], 0)
        for e in range(N_EXPERTS):
            pl.when(padded_ref[e] > 0)(lambda e=e: tail_fill(e).wait())
        lax.fori_loop(used_ref[0], n_tiles, lambda j, c: (zero_tile(j * tm).wait(), c)[1], 0)

    def copy(r, k):
        return pltpu.make_async_copy(src_ref.at[pl.ds(r, 1)],
                                     o_ref.at[pl.ds(pos_ref[0, TOP_K * r + k], 1)], sem)

    def issue(r, c):
        for k in range(TOP_K):
            copy(r, k).start()
        return c

    def drain(r, c):
        for k in range(TOP_K):
            copy(r, k).wait()
        return c

    lax.fori_loop(0, tokens, issue, 0, unroll=GATHER_UNROLL)
    lax.fori_loop(0, tokens, drain, 0, unroll=GATHER_UNROLL)


def _dispatch(src, pos, starts, padded, n_used, n_rows, tm):
    t, width = src.shape
    tg = min(GATHER_TILE, t)
    grid_spec = pltpu.PrefetchScalarGridSpec(
        num_scalar_prefetch=3,
        grid=(t // tg,),
        in_specs=[pl.BlockSpec((None, 1, TOP_K * tg), lambda i, s, p, u: (i, 0, 0), memory_space=pltpu.SMEM),
                  pl.BlockSpec((tg, width), lambda i, s, p, u: (i, 0)),
                  pl.BlockSpec((tm, width), lambda i, s, p, u: (0, 0), pipeline_mode=pl.Buffered(1))],
        out_specs=pl.BlockSpec(memory_space=pl.ANY),
        scratch_shapes=[pltpu.SemaphoreType.DMA(()), pltpu.SemaphoreType.DMA(())],
    )
    return pl.pallas_call(
        _dispatch_kernel,
        grid_spec=grid_spec,
        out_shape=jax.ShapeDtypeStruct((n_rows, width), src.dtype),
        compiler_params=_cparams(("arbitrary",)),
        name="moe_dispatch",
    )(starts, padded, n_used, pos.reshape(t // tg, 1, TOP_K * tg), src, jnp.zeros((tm, width), src.dtype))


def _expert_kernel(te_ref, tv_ref, tx_ref, x_ref, w1_ref, w3_ref, w2_ref, o_ref, xb_ref):
    i = pl.program_id(0)
    f = pl.program_id(1)

    @pl.when(f == 0)
    def _():
        xb_ref[...] = x_ref[...].astype(BF16)
        o_ref[...] = jnp.zeros_like(o_ref)

    @pl.when(tv_ref[i] > 0)
    def _():
        xb = xb_ref[...]
        a1 = jnp.dot(xb, w1_ref[...], preferred_element_type=F32)
        a3 = jnp.dot(xb, w3_ref[...], preferred_element_type=F32)
        act = (a1 * jax.nn.sigmoid(a1) * a3).astype(BF16)
        o_ref[...] += jnp.dot(act, w2_ref[...], preferred_element_type=F32)


def _experts(xs, tile_expert, tile_valid, tile_src, w1, w3, w2, tm):
    n = xs.shape[0]
    d_ff = w1.shape[2]
    tf = EXPERT_COL_TILE
    nf = d_ff // tf
    fidx = lambda i, f, tv: jnp.where(tv[i] > 0, f, nf - 1)
    grid_spec = pltpu.PrefetchScalarGridSpec(
        num_scalar_prefetch=3,
        grid=(n // tm, nf),
        in_specs=[pl.BlockSpec((tm, D_MODEL), lambda i, f, te, tv, tx: (tx[i], 0)),
                  pl.BlockSpec((None, D_MODEL, tf), lambda i, f, te, tv, tx: (te[i], 0, fidx(i, f, tv))),
                  pl.BlockSpec((None, D_MODEL, tf), lambda i, f, te, tv, tx: (te[i], 0, fidx(i, f, tv))),
                  pl.BlockSpec((None, tf, D_MODEL), lambda i, f, te, tv, tx: (te[i], fidx(i, f, tv), 0))],
        out_specs=pl.BlockSpec((tm, D_MODEL), lambda i, f, te, tv, tx: (i, 0)),
        scratch_shapes=[pltpu.VMEM((tm, D_MODEL), BF16)],
    )
    return pl.pallas_call(
        _expert_kernel,
        grid_spec=grid_spec,
        out_shape=jax.ShapeDtypeStruct((n, D_MODEL), F32),
        compiler_params=_cparams(("arbitrary", "arbitrary")),
        name="moe_experts",
    )(tile_expert, tile_valid, tile_src, xs, w1, w3, w2)


def _combine_kernel(p0_ref, p1_ref, h_ref, gate_ref, ys_ref, o_ref, b0, b1, sem0, sem1):
    rows = o_ref.shape[0]

    def copies(r):
        return (pltpu.make_async_copy(ys_ref.at[pl.ds(p0_ref[0, r], 1)], b0.at[pl.ds(r, 1)], sem0),
                pltpu.make_async_copy(ys_ref.at[pl.ds(p1_ref[0, r], 1)], b1.at[pl.ds(r, 1)], sem1))

    def issue(r, c):
        c0, c1 = copies(r)
        c0.start()
        c1.start()
        return c

    def drain(r, c):
        c0, c1 = copies(r)
        c0.wait()
        c1.wait()
        return c

    lax.fori_loop(0, rows, issue, 0, unroll=GATHER_UNROLL)
    lax.fori_loop(0, rows, drain, 0, unroll=GATHER_UNROLL)
    gate = gate_ref[...]
    o_ref[...] = h_ref[...] + gate[:, 0:1] * b0[...] + gate[:, 1:2] * b1[...]


def _combine(h, gates, ys, pos0, pos1):
    t = h.shape[0]
    tg = min(GATHER_TILE, t)
    smem = pl.BlockSpec((None, 1, tg), lambda i: (i, 0, 0), memory_space=pltpu.SMEM)
    return pl.pallas_call(
        _combine_kernel,
        grid=(t // tg,),
        in_specs=[smem, smem,
                  pl.BlockSpec((tg, D_MODEL), lambda i: (i, 0)),
                  pl.BlockSpec((tg, ROUTER_LANES), lambda i: (i, 0)),
                  pl.BlockSpec(memory_space=pl.ANY)],
        out_specs=pl.BlockSpec((tg, D_MODEL), lambda i: (i, 0)),
        out_shape=jax.ShapeDtypeStruct((t, D_MODEL), F32),
        scratch_shapes=[pltpu.VMEM((tg, D_MODEL), F32), pltpu.VMEM((tg, D_MODEL), F32),
                        pltpu.SemaphoreType.DMA(()), pltpu.SemaphoreType.DMA(())],
        compiler_params=_cparams(("arbitrary",)),
        name="moe_combine",
    )(pos0.reshape(t // tg, 1, tg), pos1.reshape(t // tg, 1, tg), h, gates, ys)


def _moe(h, hn, idx, gates, w1, w3, w2):
    t = h.shape[0]
    tm = min(ROW_TILE, t)
    n_tiles = (t * TOP_K) // tm + N_EXPERTS
    n_rows = n_tiles * tm
    flat_e = idx[:, :TOP_K].reshape(-1)
    onehot = (flat_e[:, None] == jnp.arange(N_EXPERTS)[None, :]).astype(jnp.int32)
    rank = jnp.cumsum(onehot, axis=0) - onehot
    counts = jnp.sum(onehot, axis=0)
    padded = ((counts + tm - 1) // tm) * tm
    ends = jnp.cumsum(padded)
    starts = ends - padded
    pos = jnp.sum(onehot * (starts[None, :] + rank), axis=1).astype(jnp.int32)
    tile_idx = jnp.arange(n_tiles, dtype=jnp.int32)
    tile_start = tile_idx * tm
    tile_valid = (tile_start < ends[-1]).astype(jnp.int32)
    tile_expert = jnp.minimum(jnp.sum((tile_start[:, None] >= ends[None, :]).astype(jnp.int32), axis=1),
                              N_EXPERTS - 1)
    last_tile = jnp.maximum(ends[-1] // tm - 1, 0).astype(jnp.int32)
    tile_expert = jnp.where(tile_valid > 0, tile_expert, tile_expert[last_tile]).astype(jnp.int32)
    tile_src = jnp.where(tile_valid > 0, tile_idx, last_tile)
    n_used = (ends[-1:] // tm).astype(jnp.int32)
    xs = _dispatch(hn, pos, starts.astype(jnp.int32), padded.astype(jnp.int32), n_used, n_rows, tm)
    ys = _experts(xs, tile_expert, tile_valid, tile_src, w1, w3, w2, tm)
    pos = pos.reshape(t, TOP_K)
    return _combine(h, gates, ys, pos[:, 0], pos[:, 1])


def _permute_in_cols(w):
    o = V_END
    wd = RWKV_WIDTH
    r = (o, o + wd)
    xw = (r[1], r[1] + RWKV_DECAY_LORA)
    k = (xw[1], xw[1] + wd)
    v = (k[1], k[1] + wd)
    xa = (v[1], v[1] + RWKV_A_LORA)
    xg = (xa[1], xa[1] + RWKV_GATE_LORA)
    order = [(0, o), r, k, v, xg, xw, xa, (RWKV_END, IN_COLS)]
    return jnp.concatenate([w[..., a:b] for a, b in order], axis=-1)


def kernel(x, norm1_g, w_in, q_norm_g, k_norm_g, attn_sinks, rel_bias, rwkv_mu, rwkv_w0, rwkv_w2, rwkv_a0, rwkv_a2, rwkv_g2, rwkv_k_k, rwkv_k_a, rwkv_r_k, rwkv_ln_w, rwkv_ln_b, ssm_lambda_re, ssm_lambda_im, ssm_b_re, ssm_b_im, ssm_c_re, ssm_c_im, ssm_d, ssm_log_dt, ssm_glu_w, ssm_glu_b, attn_out_g, ssm_out_g, w_out, norm2_g, ffn_w1, ffn_w3, ffn_w2, moe_router, moe_w1, moe_w3, moe_w2):
    batch, seq, _ = x.shape
    depth = w_in.shape[0]
    h = x.reshape(batch * seq, D_MODEL)
    for i in range(depth):
        w_in_i = _permute_in_cols(w_in[i]).astype(BF16)
        mu_i = _permute_in_cols(jnp.pad(rwkv_mu[i], (V_END, SSM_WIDTH)))[V_END:RWKV_END]
        q, kv, zr, zs = _in_proj(h, norm1_g[i], w_in_i)
        attn = _attention(q, kv, batch, seq, q_norm_g[i], k_norm_g[i], attn_sinks[i], rel_bias,
                          attn_out_g[i])
        rw = _rwkv(zr, batch, seq, mu_i, rwkv_w0[i], rwkv_w2[i], rwkv_a0[i], rwkv_a2[i], rwkv_g2[i],
                   rwkv_k_k[i], rwkv_k_a[i], rwkv_r_k[i], rwkv_ln_w[i], rwkv_ln_b[i])
        ss = _s5(zs, batch, seq, ssm_lambda_re[i], ssm_lambda_im[i], ssm_b_re[i], ssm_b_im[i],
                 ssm_c_re[i], ssm_c_im[i], ssm_d[i], ssm_log_dt[i], ssm_glu_w[i], ssm_glu_b[i],
                 ssm_out_g[i])
        j = i // 2
        if i % 2 == 0:
            h, hn = _out_proj(h, attn, rw, ss, w_out[i].astype(BF16), norm2_g[i])
            h = _ffn(h, hn, ffn_w1[j].astype(BF16), ffn_w3[j].astype(BF16), ffn_w2[j].astype(BF16))
        else:
            h, hn, idx, gates = _out_proj(h, attn, rw, ss, w_out[i].astype(BF16), norm2_g[i],
                                          router=moe_router[j])
            h = _moe(h, hn, idx, gates, moe_w1[j].astype(BF16), moe_w3[j].astype(BF16),
                     moe_w2[j].astype(BF16))
    return h.reshape(batch, seq, D_MODEL)
```

```python
import functools
import math

import numpy as np
import jax
import jax.numpy as jnp
from jax import lax
from jax.experimental import pallas as pl
from jax.experimental.pallas import tpu as pltpu

F32 = jnp.float32
BF16 = jnp.bfloat16

D_MODEL = 2048
HEAD_DIM = 64
ATTN_WIDTH = 1024
RWKV_WIDTH = 512
SSM_WIDTH = 512
ATTN_HEADS = 16
ATTN_KV_HEADS = 4
ATTN_GROUP = 4
KV_WIDTH = 256
WINDOW = 128
BLOCK = 128
REL_BUCKETS = 32
REL_MAX_DISTANCE = 128
RWKV_HEADS = 8
RWKV_DECAY_LORA = 64
RWKV_A_LORA = 64
RWKV_GATE_LORA = 128
RWKV_COLS = 1792
RWKV_DECAY_SCALE = math.exp(-0.5)
RWKV_LN_EPS = HEAD_DIM * 1e-5
SSM_GROUP_CH = 16
SSM_GROUPS = 32
SSM_STATE = 64
SSM_STATES = SSM_GROUPS * SSM_STATE
Q_END = ATTN_WIDTH
K_END = Q_END + KV_WIDTH
V_END = K_END + KV_WIDTH
RWKV_END = V_END + RWKV_COLS
IN_COLS = RWKV_END + SSM_WIDTH
D_FF = 5632
N_EXPERTS = 8
TOP_K = 2
D_FF_EXPERT = 7168
NORM_EPS = 1e-6

LANES = 128
MXU_DIM = 256
VMEM_LIMIT = 56 * 1024 * 1024

ROW_TILE = 512
FFN_COL_TILE = 512
EXPERT_COL_TILE = 1024
SCAN_TILE = 256
S5_SEGMENTS = 8
S5_STRIP = 512
RWKV_CHUNK = 64
RWKV_QUAD = 4
GATHER_TILE = 512
GATHER_UNROLL = 8
ROUTER_LANES = 128
ROUTER_SUBBLOCKS = 4


def _cparams(sem):
    return pltpu.CompilerParams(dimension_semantics=sem, vmem_limit_bytes=VMEM_LIMIT)


def _mm(a, b):
    return jnp.dot(a.astype(BF16), b.astype(BF16), preferred_element_type=F32)


def _mm_nt(a, b):
    return lax.dot_general(a.astype(BF16), b.astype(BF16), (((1,), (1,)), ((), ())),
                           preferred_element_type=F32)


def _mm_tn(a, b):
    return lax.dot_general(a.astype(BF16), b.astype(BF16), (((0,), (0,)), ((), ())),
                           preferred_element_type=F32)


def _split2(x):
    hi = x.astype(BF16)
    lo = (x - hi.astype(F32)).astype(BF16)
    return hi, lo


def _split3(x):
    h1 = x.astype(BF16)
    r1 = x - h1.astype(F32)
    h2 = r1.astype(BF16)
    h3 = (r1 - h2.astype(F32)).astype(BF16)
    return h1, h2, h3


def _seg_sum(x, ones_bd):
    hi, lo = _split2(x)
    return (jnp.dot(hi, ones_bd, preferred_element_type=F32)
            + jnp.dot(lo, ones_bd, preferred_element_type=F32))


def _const_spec(shape):
    nd = len(shape)
    return pl.BlockSpec(shape, lambda *_: (0,) * nd, pipeline_mode=pl.Buffered(1))


def _in_proj_kernel(x_ref, g_ref, w_ref, q_ref, kv_ref, rw_ref, ss_ref):
    x = x_ref[...]
    ms = jnp.mean(x * x, axis=-1, keepdims=True)
    hn = (x * lax.rsqrt(ms + NORM_EPS) * g_ref[...]).astype(BF16)
    q_ref[...] = jnp.dot(hn, w_ref[:, 0:Q_END], preferred_element_type=F32)
    kv_ref[...] = jnp.dot(hn, w_ref[:, Q_END:V_END], preferred_element_type=F32)
    rw_ref[...] = jnp.dot(hn, w_ref[:, V_END:RWKV_END], preferred_element_type=F32)
    ss_ref[...] = jnp.dot(hn, w_ref[:, RWKV_END:IN_COLS], preferred_element_type=F32)


def _in_proj(h, g, w):
    t = h.shape[0]
    tm = min(ROW_TILE, t)
    row = lambda w_: pl.BlockSpec((tm, w_), lambda i: (i, 0))
    return pl.pallas_call(
        _in_proj_kernel,
        grid=(t // tm,),
        in_specs=[row(D_MODEL), _const_spec((1, D_MODEL)), _const_spec((D_MODEL, IN_COLS))],
        out_specs=[row(ATTN_WIDTH), row(2 * KV_WIDTH), row(RWKV_COLS), row(SSM_WIDTH)],
        out_shape=[jax.ShapeDtypeStruct((t, ATTN_WIDTH), F32),
                   jax.ShapeDtypeStruct((t, 2 * KV_WIDTH), F32),
                   jax.ShapeDtypeStruct((t, RWKV_COLS), F32),
                   jax.ShapeDtypeStruct((t, SSM_WIDTH), F32)],
        compiler_params=_cparams(("arbitrary",)),
        name="in_proj",
    )(h, g.reshape(1, D_MODEL), w)


def _attn_kernel(sink_ref, q_ref, kvp_ref, kvc_ref, bias_ref, qg_ref, kg_ref, og_ref, e_ref, dup_ref,
                 o_ref, acc_ref):
    first = pl.program_id(1) == 0
    e = e_ref[...]
    kv = jnp.concatenate([kvp_ref[...], kvc_ref[...]], axis=0)
    rows = ATTN_GROUP * BLOCK

    def head_norm(t, g):
        ss = _seg_sum(t * t, e)
        return t * lax.rsqrt(ss * (1.0 / HEAD_DIM) + NORM_EPS) * g

    kn = head_norm(kv[:, :KV_WIDTH], kg_ref[...]).astype(BF16)
    v_dup = jnp.dot(kv[:, KV_WIDTH:].astype(BF16), dup_ref[...], preferred_element_type=F32).astype(BF16)
    ones = jnp.ones((2 * BLOCK, LANES), BF16)
    key_idx = lax.broadcasted_iota(jnp.int32, (rows, 2 * BLOCK), 1)
    no_prev = jnp.logical_and(first, key_idx < BLOCK)
    group = lax.broadcasted_iota(jnp.int32, (rows, 1), 0) // BLOCK
    low_half = lax.broadcasted_iota(jnp.int32, (1, LANES), 1) < HEAD_DIM
    scale = HEAD_DIM ** -0.5

    def scores(h):
        qn = (head_norm(q_ref[:, h * MXU_DIM:(h + 1) * MXU_DIM], qg_ref[...]) * scale).astype(BF16)
        qs = jnp.concatenate([qn[:, g * HEAD_DIM:(g + 1) * HEAD_DIM] for g in range(ATTN_GROUP)], axis=0)
        bias = bias_ref[h * ATTN_GROUP:(h + 1) * ATTN_GROUP].reshape(rows, 2 * BLOCK)
        return jnp.where(no_prev, -jnp.inf, _mm_nt(qs, kn[:, h * HEAD_DIM:(h + 1) * HEAD_DIM]) + bias)

    def finish(h, s):
        sink = jnp.full((rows, 1), sink_ref[h * ATTN_GROUP], F32)
        for g in range(1, ATTN_GROUP):
            sink = jnp.where(group == g, sink_ref[h * ATTN_GROUP + g], sink)
        m = jnp.maximum(jnp.max(s, axis=-1, keepdims=True), sink)
        p = jnp.exp(s - m).astype(BF16)
        rhs = jnp.concatenate([v_dup[:, h * LANES:(h + 1) * LANES], ones], axis=1)
        pv = jnp.dot(p, rhs, preferred_element_type=F32)
        out = pv[:, :LANES] / (pv[:, LANES:] + jnp.exp(sink - m))
        for pair in range(ATTN_GROUP // 2):
            even = out[(2 * pair) * BLOCK:(2 * pair + 1) * BLOCK]
            odd = out[(2 * pair + 1) * BLOCK:(2 * pair + 2) * BLOCK]
            col = (h * ATTN_GROUP + 2 * pair) * HEAD_DIM
            acc_ref[:, col:col + LANES] = jnp.where(low_half, even, odd)

    pending = scores(0)
    for h in range(ATTN_KV_HEADS):
        nxt = scores(h + 1) if h + 1 < ATTN_KV_HEADS else None
        finish(h, pending)
        pending = nxt
    o = acc_ref[...]
    ms = jnp.mean(o * o, axis=-1, keepdims=True)
    o_ref[...] = (o * lax.rsqrt(ms + NORM_EPS) * og_ref[...]).astype(BF16)


def _t5_bucket_table():
    qi = np.arange(BLOCK)[:, None]
    kj = np.arange(2 * BLOCK)[None, :]
    dist = BLOCK + qi - kj
    max_exact = REL_BUCKETS // 2
    d = np.maximum(dist, 0)
    large = max_exact + (np.log(np.maximum(d, 1).astype(np.float32) / max_exact)
                         / math.log(REL_MAX_DISTANCE / max_exact)
                         * (REL_BUCKETS - max_exact)).astype(np.int32)
    large = np.minimum(large, REL_BUCKETS - 1)
    bucket = np.where(d < max_exact, d, large)
    valid = (dist >= 0) & (dist < WINDOW)
    return bucket, valid


def _attention(q, kv, batch, seq, q_g, k_g, sinks, rel_bias, out_g):
    nb = seq // BLOCK
    bucket, valid = _t5_bucket_table()
    bias = jnp.transpose(rel_bias.astype(F32)[bucket], (2, 0, 1))
    bias = jnp.where(valid[None], bias, -jnp.inf)
    seg = np.arange(MXU_DIM) // HEAD_DIM
    ones_bd = jnp.asarray(seg[:, None] == seg[None, :], dtype=BF16)
    dup_src = (np.arange(2 * KV_WIDTH) // LANES) * HEAD_DIM + np.arange(2 * KV_WIDTH) % HEAD_DIM
    dup = jnp.asarray(np.arange(KV_WIDTH)[:, None] == dup_src[None, :], dtype=BF16)
    tile4 = lambda g: jnp.tile(g.astype(F32), ATTN_GROUP).reshape(1, MXU_DIM)
    grid_spec = pltpu.PrefetchScalarGridSpec(
        num_scalar_prefetch=1,
        grid=(batch, nb),
        in_specs=[
            pl.BlockSpec((BLOCK, ATTN_WIDTH), lambda b, n, s: (b * nb + n, 0)),
            pl.BlockSpec((BLOCK, 2 * KV_WIDTH), lambda b, n, s: (jnp.maximum(b * nb + n - 1, 0), 0)),
            pl.BlockSpec((BLOCK, 2 * KV_WIDTH), lambda b, n, s: (b * nb + n, 0)),
            pl.BlockSpec((ATTN_HEADS, BLOCK, 2 * BLOCK), lambda b, n, s: (0, 0, 0)),
            pl.BlockSpec((1, MXU_DIM), lambda b, n, s: (0, 0)),
            pl.BlockSpec((1, MXU_DIM), lambda b, n, s: (0, 0)),
            pl.BlockSpec((1, ATTN_WIDTH), lambda b, n, s: (0, 0)),
            pl.BlockSpec((MXU_DIM, MXU_DIM), lambda b, n, s: (0, 0)),
            pl.BlockSpec((KV_WIDTH, 2 * KV_WIDTH), lambda b, n, s: (0, 0)),
        ],
        out_specs=pl.BlockSpec((BLOCK, ATTN_WIDTH), lambda b, n, s: (b * nb + n, 0)),
        scratch_shapes=[pltpu.VMEM((BLOCK, ATTN_WIDTH), F32)],
    )
    return pl.pallas_call(
        _attn_kernel,
        grid_spec=grid_spec,
        out_shape=jax.ShapeDtypeStruct((batch * seq, ATTN_WIDTH), BF16),
        compiler_params=_cparams(("arbitrary", "arbitrary")),
        name="swa",
    )(sinks.astype(F32), q, kv, kv, bias, tile4(q_g), tile4(k_g), out_g.reshape(1, ATTN_WIDTH), ones_bd,
      dup)


def _rwkv_consts():
    c, g_heads = RWKV_CHUNK, RWKV_QUAD
    w = g_heads * HEAD_DIM
    row = lax.broadcasted_iota(jnp.int32, (c, g_heads * c), 0)
    col = lax.broadcasted_iota(jnp.int32, (c, g_heads * c), 1) % c
    strict = col < row
    eye = (row == col).astype(F32)
    diag2 = jnp.logical_and((row // 2) == (col // 2), strict)
    levels = []
    bs = 2
    while bs < c:
        levels.append(((row // (2 * bs)) == (col // (2 * bs))) & ((row // bs) % 2 == 1)
                      & ((col // bs) % 2 == 0))
        bs *= 2
    hrow = lax.broadcasted_iota(jnp.int32, (w, w), 0) // HEAD_DIM
    hcol = lax.broadcasted_iota(jnp.int32, (w, w), 1) // HEAD_DIM
    return strict, col <= row, eye, diag2, levels, hrow == hcol


def _rwkv_stack(x, bm):
    return jnp.tile(x.astype(BF16), (RWKV_QUAD, 1)) * bm


def _rwkv_prepare(items, bm, consts):
    strict, incl, eye, diag2, levels, _ = consts
    c = RWKV_CHUNK
    gc = RWKV_QUAD * c
    w = RWKV_QUAD * HEAD_DIM
    nt = (((1,), (1,)), ((), ()))

    def elementwise(d):
        r, lw, cum, k, v, a, b = d.pop("item")
        last = cum[c - 1:c]
        p_inv = jnp.exp(-cum)
        ratio = jnp.exp(last - cum)
        a_t = a * jnp.exp(cum - lw)
        d.update(
            ar=jnp.concatenate([a_t, r * jnp.exp(cum)], axis=0).astype(BF16),
            bks=jnp.concatenate([_rwkv_stack(b * p_inv, bm), _rwkv_stack(k * p_inv, bm)], axis=0),
            a_stack=_rwkv_stack(a_t, bm), v=v, v_stack=_rwkv_stack(v, bm),
            uv_t=jnp.concatenate([b * ratio, k * ratio], axis=0).T.astype(BF16),
            p_col=jnp.broadcast_to(jnp.exp(last), (8, w)).T[:, 0:1])

    def scores(d):
        big = lax.dot_general(d["ar"], d.pop("bks"), nt, preferred_element_type=F32)
        d["n_cat"] = jnp.where(strict, big[:c, :gc], 0.0)
        d["t"] = eye + jnp.where(diag2, d["n_cat"], 0.0)
        d["r_b"] = jnp.where(incl, big[c:, :gc], 0.0).astype(BF16)
        d["akrk"] = jnp.concatenate([jnp.where(strict, big[:c, gc:], 0.0),
                                     jnp.where(incl, big[c:, gc:], 0.0)], axis=0).astype(BF16)

    def level_first(lm):
        def stage(d):
            d["tn"] = _mm(d["t"], _rwkv_stack(jnp.where(lm, d["n_cat"], 0.0), bm))
        return stage

    def level_second(d):
        d["t"] = d["t"] + _mm(d.pop("tn"), _rwkv_stack(d["t"], bm))

    def values(d):
        d["wy"] = jnp.dot(d.pop("akrk"), d.pop("v_stack"), preferred_element_type=F32)

    def fold_inverse(d):
        wy = d.pop("wy")
        rhs = jnp.concatenate([d.pop("a_stack"), _rwkv_stack(wy[:c], bm)], axis=1)
        tw = jnp.dot(d.pop("t").astype(BF16), rhs, preferred_element_type=F32)
        d["lhs"] = jnp.concatenate([tw[:, :w].astype(BF16), d.pop("ar")[c:]], axis=0)
        d["uc"] = tw[:, w:]
        d["yc"] = wy[c:]
        d.pop("n_cat")

    stages = [elementwise, scores]
    for lm in levels:
        stages += [level_first(lm), level_second]
    stages += [values, fold_inverse]
    out = [dict(item=it) for it in items]
    for stage in stages:
        for d in out:
            stage(d)
    return out


def _rwkv_apply(ds, hss, bm, consts, between):
    c = RWKV_CHUNK
    ohs = [jnp.dot(d["lhs"], hs.astype(BF16), preferred_element_type=F32) for d, hs in zip(ds, hss)]
    between[0]()
    us = [oh[:c] + d["uc"] for d, oh in zip(ds, ohs)]
    upds = [jnp.dot(d["uv_t"], jnp.concatenate([u, d["v"]], axis=0).astype(BF16), preferred_element_type=F32)
            for d, u in zip(ds, us)]
    between[1]()
    ys = [oh[c:] + _mm(d["r_b"], _rwkv_stack(u, bm)) + d["yc"] for d, oh, u in zip(ds, ohs, us)]
    between[2]()
    new = [hs * d["p_col"] + jnp.where(consts[5], upd, 0.0) for d, hs, upd in zip(ds, hss, upds)]
    return ys, new


def _rwkv_kernel(z_ref, mu_ref, w2a_ref, g2_ref, w0_ref, a0_ref, kk_ref, ka_ref, rk_ref, lnw_ref,
                 lnb_ref, e_ref, tri_ref, bm_ref, o_ref, carry_ref, h_s):
    tb = z_ref.shape[0]
    wd = RWKV_WIDTH
    quad_w = RWKV_QUAD * HEAD_DIM
    n_quads = RWKV_HEADS // RWKV_QUAD
    c = RWKV_CHUNK

    @pl.when(pl.program_id(1) == 0)
    def _():
        carry_ref[...] = jnp.zeros_like(carry_ref)
        h_s[...] = jnp.zeros_like(h_s)

    z = z_ref[...]
    rows = lax.broadcasted_iota(jnp.int32, (tb, 1), 0)
    prev = jnp.where(rows == 0, carry_ref[...], pltpu.roll(z, 1, axis=0))
    carry_ref[...] = z[tb - 1:tb]
    zs = z + mu_ref[...] * (prev - z)
    r = zs[:, 0:wd]
    k = zs[:, wd:2 * wd]
    v = zs[:, 2 * wd:3 * wd]
    xg = zs[:, 3 * wd:3 * wd + RWKV_GATE_LORA]
    xwa = zs[:, 3 * wd + RWKV_GATE_LORA:]
    lane = lax.broadcasted_iota(jnp.int32, (1, RWKV_DECAY_LORA + RWKV_A_LORA), 1)
    lora_in = jnp.where(lane < RWKV_DECAY_LORA, jnp.tanh(xwa), xwa)
    lora = _mm(lora_in, w2a_ref[...])
    lw = -RWKV_DECAY_SCALE * jax.nn.sigmoid(w0_ref[...] + lora[:, :wd])
    iclr = jax.nn.sigmoid(a0_ref[...] + lora[:, wd:])
    gate = _mm(jax.nn.sigmoid(xg), g2_ref[...])
    kk = k * kk_ref[...]
    e = e_ref[...]
    seg = lambda x: jnp.concatenate(
        [_seg_sum(x[:, q * quad_w:(q + 1) * quad_w], e) for q in range(n_quads)], axis=1)
    kk = kk * lax.rsqrt(jnp.maximum(seg(kk * kk), 1e-12))
    k2 = k * (1.0 + (iclr - 1.0) * ka_ref[...])
    l1, l2, l3 = _split3(lw)
    tri = tri_ref[...]
    cum = (jnp.dot(tri, l1, preferred_element_type=F32) + jnp.dot(tri, l2, preferred_element_type=F32)
           + jnp.dot(tri, l3, preferred_element_type=F32))
    a = -kk
    b = kk * iclr

    consts = _rwkv_consts()
    bm = bm_ref[...]
    n_chunks = tb // c
    items = []
    for ci in range(n_chunks):
        for q in range(n_quads):
            sl = (slice(ci * c, (ci + 1) * c), slice(q * quad_w, (q + 1) * quad_w))
            items.append((r[sl], lw[sl], cum[sl], k2[sl], v[sl], a[sl], b[sl]))
    bonus = seg(r * k2 * rk_ref[...]) * v
    prepared = _rwkv_prepare(items, bm, consts)
    lnw = lnw_ref[...]
    lnb = lnb_ref[...]

    def group_norm_stages(p):
        if p is None:
            return (lambda: None,) * 3

        def mean():
            p["mean"] = seg(p["y"]) * (1.0 / HEAD_DIM)

        def var():
            p["d"] = p["y"] - p["mean"]
            p["var"] = seg(p["d"] * p["d"]) * (1.0 / HEAD_DIM)

        def store():
            rows = p["rows"]
            yn = p["d"] * lax.rsqrt(p["var"] + RWKV_LN_EPS) * lnw + lnb
            o_ref[rows, :] = ((yn + bonus[rows]) * gate[rows]).astype(BF16)

        return mean, var, store

    hts = [h_s[q] for q in range(n_quads)]
    done = None
    for ci in range(n_chunks):
        ys, hts = _rwkv_apply(prepared[ci * n_quads:(ci + 1) * n_quads], hts, bm, consts,
                              group_norm_stages(done))
        done = dict(y=jnp.concatenate(ys, axis=1), rows=slice(ci * c, (ci + 1) * c))
    for q in range(n_quads):
        h_s[q] = hts[q]
    for stage in group_norm_stages(done):
        stage()


def _rwkv(z, batch, seq, mu, w0, w2, a0, a2, g2, k_k, k_a, r_k, ln_w, ln_b):
    tb = min(SCAN_TILE, seq)
    nb = seq // tb
    wd = RWKV_WIDTH
    quad_w = RWKV_QUAD * HEAD_DIM
    w2a = jnp.zeros((RWKV_DECAY_LORA + RWKV_A_LORA, 2 * wd), F32)
    w2a = w2a.at[:RWKV_DECAY_LORA, :wd].set(w2).at[RWKV_DECAY_LORA:, wd:].set(a2).astype(BF16)
    seg = np.arange(quad_w) // HEAD_DIM
    ones_bd = jnp.asarray(seg[:, None] == seg[None, :], dtype=BF16)
    t_idx = np.arange(tb)
    tri = jnp.asarray((t_idx[:, None] // RWKV_CHUNK == t_idx[None, :] // RWKV_CHUNK)
                      & (t_idx[None, :] <= t_idx[:, None]), dtype=BF16)
    stack_rows = np.arange(RWKV_QUAD * RWKV_CHUNK) // RWKV_CHUNK
    stack_mask = jnp.asarray(stack_rows[:, None] == seg[None, :], dtype=BF16)
    vec = lambda p: p.astype(F32).reshape(1, -1)
    return pl.pallas_call(
        _rwkv_kernel,
        grid=(batch, nb),
        in_specs=[pl.BlockSpec((tb, RWKV_COLS), lambda b, j: (b * nb + j, 0)),
                  _const_spec((1, RWKV_COLS)),
                  _const_spec((RWKV_DECAY_LORA + RWKV_A_LORA, 2 * wd)),
                  _const_spec((RWKV_GATE_LORA, wd))]
                 + [_const_spec((1, wd))] * 7
                 + [_const_spec((quad_w, quad_w)), _const_spec((tb, tb)),
                    _const_spec((RWKV_QUAD * RWKV_CHUNK, quad_w))],
        out_specs=pl.BlockSpec((tb, wd), lambda b, j: (b * nb + j, 0)),
        out_shape=jax.ShapeDtypeStruct((batch * seq, wd), BF16),
        scratch_shapes=[pltpu.VMEM((1, RWKV_COLS), F32),
                        pltpu.VMEM((RWKV_HEADS // RWKV_QUAD, quad_w, quad_w), F32)],
        compiler_params=_cparams(("arbitrary", "arbitrary")),
        name="rwkv7",
    )(z, vec(mu), w2a, g2.astype(BF16), vec(w0), vec(a0), vec(k_k), vec(k_a), vec(r_k), vec(ln_w),
      vec(ln_b), ones_bd, tri, stack_mask)


def _cmul_add(ar, ai, xr, xi, br, bi):
    return ar * xr - ai * xi + br, ar * xi + ai * xr + bi


def _s5_kernel(u_ref, perm_ref, bbr_ref, bbi_ref, cr_ref, ci_ref, a1_ref, aseg_ref, apj_ref, ask_ref,
               d_ref, gw_ref, gb_ref, og_ref, o_ref, xr_s, xi_s, car_s, cai_s):
    tb = u_ref.shape[0]
    seg_len = tb // S5_SEGMENTS
    strip = S5_STRIP
    n_strips = SSM_STATES // strip
    n_groups = SSM_WIDTH // LANES

    @pl.when(pl.program_id(1) == 0)
    def _():
        car_s[...] = jnp.zeros_like(car_s)
        cai_s[...] = jnp.zeros_like(cai_s)

    perm = perm_ref[...]
    u1, u2, u3 = _split3(u_ref[...])
    u = (jnp.dot(perm, u1, preferred_element_type=F32) + jnp.dot(perm, u2, preferred_element_type=F32)
         + jnp.dot(perm, u3, preferred_element_type=F32))
    ub = u.astype(BF16)
    sw = SSM_STATES // n_groups
    for g in range(n_groups):
        ug = ub[:, g * LANES:(g + 1) * LANES]
        xr_s[:, g * sw:(g + 1) * sw] = jnp.dot(ug, bbr_ref[g * LANES:(g + 1) * LANES, g * sw:(g + 1) * sw],
                                               preferred_element_type=F32)
        xi_s[:, g * sw:(g + 1) * sw] = jnp.dot(ug, bbi_ref[g * LANES:(g + 1) * LANES, g * sw:(g + 1) * sw],
                                               preferred_element_type=F32)

    for s in range(n_strips):
        cols = slice(s * strip, (s + 1) * strip)
        ar = jnp.broadcast_to(a1_ref[0:1, cols], (S5_SEGMENTS, strip))
        ai = jnp.broadcast_to(a1_ref[1:2, cols], (S5_SEGMENTS, strip))

        fr, fi = xr_s[0:S5_SEGMENTS, cols], xi_s[0:S5_SEGMENTS, cols]
        for j in range(1, seg_len):
            rs = slice(j * S5_SEGMENTS, (j + 1) * S5_SEGMENTS)
            fr, fi = _cmul_add(ar, ai, fr, fi, xr_s[rs, cols], xi_s[rs, cols])
            xr_s[rs, cols] = fr
            xi_s[rs, cols] = fi
        seg_row = lax.broadcasted_iota(jnp.int32, (S5_SEGMENTS, strip), 0)
        ir, ii = fr, fi
        for lvl in range(int(math.log2(S5_SEGMENTS))):
            sh = 1 << lvl
            pr = jnp.where(seg_row >= sh, pltpu.roll(ir, sh, axis=0), 0.0)
            pi = jnp.where(seg_row >= sh, pltpu.roll(ii, sh, axis=0), 0.0)
            ir, ii = _cmul_add(aseg_ref[2 * lvl:2 * lvl + 1, cols], aseg_ref[2 * lvl + 1:2 * lvl + 2, cols],
                               pr, pi, ir, ii)
        c_in_r = car_s[:, cols]
        c_in_i = cai_s[:, cols]
        er = jnp.where(seg_row >= 1, pltpu.roll(ir, 1, axis=0), 0.0)
        ei = jnp.where(seg_row >= 1, pltpu.roll(ii, 1, axis=0), 0.0)
        cr_, ci_ = _cmul_add(ask_ref[0:S5_SEGMENTS, cols], ask_ref[S5_SEGMENTS:2 * S5_SEGMENTS, cols],
                             c_in_r, c_in_i, er, ei)
        last = S5_SEGMENTS - 1
        nr, ni = _cmul_add(aseg_ref[0:1, cols], aseg_ref[1:2, cols],
                           cr_[last:last + 1], ci_[last:last + 1], fr[last:last + 1], fi[last:last + 1])
        car_s[:, cols] = nr
        cai_s[:, cols] = ni

        for j in range(seg_len):
            rs = slice(j * S5_SEGMENTS, (j + 1) * S5_SEGMENTS)
            nr, ni = _cmul_add(apj_ref[2 * j:2 * j + 1, cols], apj_ref[2 * j + 1:2 * j + 2, cols], cr_, ci_,
                               xr_s[rs, cols], xi_s[rs, cols])
            xr_s[rs, cols] = nr
            xi_s[rs, cols] = ni

    sg = SSM_STATES // n_groups
    ys = []
    for g in range(n_groups):
        xr = xr_s[:, g * sg:(g + 1) * sg].astype(BF16)
        xi = xi_s[:, g * sg:(g + 1) * sg].astype(BF16)
        ys.append(jnp.dot(xr, cr_ref[g * sg:(g + 1) * sg, g * LANES:(g + 1) * LANES], preferred_element_type=F32)
                  - jnp.dot(xi, ci_ref[g * sg:(g + 1) * sg, g * LANES:(g + 1) * LANES], preferred_element_type=F32))
    y = jnp.concatenate(ys, axis=1) + d_ref[...] * u
    zg = jax.nn.gelu(y)
    gl = _mm(zg, gw_ref[...]) + gb_ref[...]
    out = zg * jax.nn.sigmoid(gl)
    ms = jnp.mean(out * out, axis=-1, keepdims=True)
    out = (out * lax.rsqrt(ms + NORM_EPS) * og_ref[...]).astype(BF16)
    o_ref[...] = lax.dot_general(perm, out, (((0,), (0,)), ((), ())),
                                 preferred_element_type=F32).astype(BF16)


def _s5(u, batch, seq, lam_re, lam_im, b_re, b_im, c_re, c_im, d, log_dt, glu_w, glu_b, out_g):
    tb = min(SCAN_TILE, seq)
    nb = seq // tb
    seg_len = tb // S5_SEGMENTS
    f = lambda p: p.astype(F32)
    lr, li = f(lam_re), f(lam_im)
    dt = jnp.exp(f(log_dt))[:, None]
    mag = jnp.exp(lr * dt)
    ab_re = mag * jnp.cos(li * dt)
    ab_im = mag * jnp.sin(li * dt)
    den = lr * lr + li * li
    fr = ((ab_re - 1.0) * lr + ab_im * li) / den
    fi = (ab_im * lr - (ab_re - 1.0) * li) / den
    bb_re = fr[..., None] * f(b_re) - fi[..., None] * f(b_im)
    bb_im = fr[..., None] * f(b_im) + fi[..., None] * f(b_re)
    eye = jnp.eye(SSM_GROUPS, dtype=F32)
    in_bd = lambda t: jnp.einsum('gpm,gh->gmhp', t, eye).reshape(SSM_WIDTH, SSM_STATES).astype(BF16)
    out_bd = lambda t: jnp.einsum('gmp,gh->gphm', f(t), eye).reshape(SSM_STATES, SSM_WIDTH).astype(BF16)

    def a_pow(n):
        n = jnp.asarray(n, F32)[:, None, None]
        m = jnp.exp(n * (lr * dt))
        ang = n * (li * dt)
        both = jnp.stack([m * jnp.cos(ang), m * jnp.sin(ang)], axis=1)
        return both.reshape(-1, SSM_STATES)

    a1 = a_pow([1])
    aseg = a_pow([seg_len * (1 << l) for l in range(4)])
    apj = a_pow(np.arange(1, seg_len + 1))
    ask_both = a_pow(np.arange(S5_SEGMENTS) * seg_len).reshape(S5_SEGMENTS, 2, SSM_STATES)
    ask = jnp.concatenate([ask_both[:, 0], ask_both[:, 1]], axis=0)
    t_idx = np.arange(tb)
    src_time = (t_idx % S5_SEGMENTS) * seg_len + t_idx // S5_SEGMENTS
    perm = jnp.asarray(src_time[:, None] == t_idx[None, :], dtype=BF16)
    vec = lambda p: f(p).reshape(1, -1)
    buf = lambda: pltpu.VMEM((tb, SSM_STATES), F32)
    return pl.pallas_call(
        _s5_kernel,
        grid=(batch, nb),
        in_specs=[pl.BlockSpec((tb, SSM_WIDTH), lambda b, j: (b * nb + j, 0)),
                  _const_spec((tb, tb)),
                  _const_spec((SSM_WIDTH, SSM_STATES)), _const_spec((SSM_WIDTH, SSM_STATES)),
                  _const_spec((SSM_STATES, SSM_WIDTH)), _const_spec((SSM_STATES, SSM_WIDTH)),
                  _const_spec((2, SSM_STATES)), _const_spec((8, SSM_STATES)),
                  _const_spec((2 * seg_len, SSM_STATES)), _const_spec((2 * S5_SEGMENTS, SSM_STATES)),
                  _const_spec((1, SSM_WIDTH)), _const_spec((SSM_WIDTH, SSM_WIDTH)),
                  _const_spec((1, SSM_WIDTH)), _const_spec((1, SSM_WIDTH))],
        out_specs=pl.BlockSpec((tb, SSM_WIDTH), lambda b, j: (b * nb + j, 0)),
        out_shape=jax.ShapeDtypeStruct((batch * seq, SSM_WIDTH), BF16),
        scratch_shapes=[buf(), buf(),
                        pltpu.VMEM((1, SSM_STATES), F32), pltpu.VMEM((1, SSM_STATES), F32)],
        compiler_params=_cparams(("arbitrary", "arbitrary")),
        name="s5",
    )(u, perm, in_bd(bb_re), in_bd(bb_im), out_bd(c_re), out_bd(c_im), a1, aseg, apj, ask, vec(d),
      glu_w.astype(BF16), vec(glu_b), vec(out_g))


def _out_proj_kernel(h_ref, at_ref, rw_ref, ss_ref, w_ref, g_ref, *rest, with_router):
    if with_router:
        rt_ref, ho_ref, hn_ref, idx_ref, gate_ref = rest
    else:
        ho_ref, hn_ref = rest

    def project(rows):
        mix = (jnp.dot(at_ref[rows, :], w_ref[0:ATTN_WIDTH, :], preferred_element_type=F32)
               + jnp.dot(rw_ref[rows, :], w_ref[ATTN_WIDTH:ATTN_WIDTH + RWKV_WIDTH, :],
                         preferred_element_type=F32)
               + jnp.dot(ss_ref[rows, :], w_ref[ATTN_WIDTH + RWKV_WIDTH:, :], preferred_element_type=F32))
        h = h_ref[rows, :] + mix
        ho_ref[rows, :] = h
        ms = jnp.mean(h * h, axis=-1, keepdims=True)
        hn = h * lax.rsqrt(ms + NORM_EPS) * g_ref[...]
        hn_ref[rows, :] = hn.astype(hn_ref.dtype)
        return hn

    def route(rows, hn):
        h_hi, h_lo = _split2(hn)
        r_hi, r_lo = _split2(rt_ref[...])
        logits = (jnp.dot(h_hi, r_hi, preferred_element_type=F32)
                  + jnp.dot(h_lo, r_hi, preferred_element_type=F32)
                  + jnp.dot(h_hi, r_lo, preferred_element_type=F32))
        lane_i = lax.broadcasted_iota(jnp.int32, logits.shape, 1)
        lane = lane_i.astype(F32)
        logits = jnp.where(lane_i < N_EXPERTS, logits, -jnp.inf)
        m1 = jnp.max(logits, axis=-1, keepdims=True)
        i1 = jnp.min(jnp.where(logits == m1, lane, float(ROUTER_LANES)), axis=-1, keepdims=True)
        rest_l = jnp.where(lane == i1, -jnp.inf, logits)
        m2 = jnp.max(rest_l, axis=-1, keepdims=True)
        i2 = jnp.min(jnp.where(rest_l == m2, lane, float(ROUTER_LANES)), axis=-1, keepdims=True)
        e2 = jnp.exp(m2 - m1)
        g1 = 1.0 / (1.0 + e2)
        g2 = e2 / (1.0 + e2)
        idx_ref[rows, :] = jnp.where(lane_i == 0, i1, jnp.where(lane_i == 1, i2, 0.0)).astype(jnp.int32)
        gate_ref[rows, :] = jnp.where(lane_i == 0, g1, jnp.where(lane_i == 1, g2, 0.0))

    if not with_router:
        project(slice(None))
        return
    sub = h_ref.shape[0] // ROUTER_SUBBLOCKS
    pending = None
    for j in range(ROUTER_SUBBLOCKS):
        rows = slice(j * sub, (j + 1) * sub)
        hn = project(rows)
        if pending is not None:
            route(*pending)
        pending = (rows, hn)
    route(*pending)


def _out_proj(h, attn, rw, ss, w_out, g2, router=None):
    t = h.shape[0]
    tm = min(ROW_TILE, t)
    row = lambda w_: pl.BlockSpec((tm, w_), lambda i: (i, 0))
    with_router = router is not None
    in_specs = [row(D_MODEL), row(ATTN_WIDTH), row(RWKV_WIDTH), row(SSM_WIDTH),
                _const_spec((D_MODEL, D_MODEL)), _const_spec((1, D_MODEL))]
    args = [h, attn, rw, ss, w_out, g2.reshape(1, D_MODEL)]
    out_specs = [row(D_MODEL), row(D_MODEL)]
    out_shape = [jax.ShapeDtypeStruct((t, D_MODEL), F32),
                 jax.ShapeDtypeStruct((t, D_MODEL), F32 if with_router else BF16)]
    if with_router:
        rt = jnp.zeros((D_MODEL, ROUTER_LANES), F32).at[:, :N_EXPERTS].set(router.astype(F32))
        in_specs.append(_const_spec((D_MODEL, ROUTER_LANES)))
        args.append(rt)
        out_specs += [row(ROUTER_LANES), row(ROUTER_LANES)]
        out_shape += [jax.ShapeDtypeStruct((t, ROUTER_LANES), jnp.int32),
                      jax.ShapeDtypeStruct((t, ROUTER_LANES), F32)]
    return pl.pallas_call(
        functools.partial(_out_proj_kernel, with_router=with_router),
        grid=(t // tm,),
        in_specs=in_specs, out_specs=out_specs, out_shape=out_shape,
        compiler_params=_cparams(("arbitrary",)),
        name="out_proj_router" if with_router else "out_proj",
    )(*args)


def _ffn_kernel(h_ref, hn_ref, w1_ref, w3_ref, w2_ref, o_ref):
    f = pl.program_id(1)

    @pl.when(f == 0)
    def _():
        o_ref[...] = h_ref[...]

    hn = hn_ref[...]
    a1 = jnp.dot(hn, w1_ref[...], preferred_element_type=F32)
    a3 = jnp.dot(hn, w3_ref[...], preferred_element_type=F32)
    act = (a1 * jax.nn.sigmoid(a1) * a3).astype(BF16)
    o_ref[...] += jnp.dot(act, w2_ref[...], preferred_element_type=F32)


def _ffn(h, hn, w1, w3, w2):
    t = h.shape[0]
    tm = min(ROW_TILE, t)
    d_ff = w1.shape[1]
    tf = FFN_COL_TILE
    return pl.pallas_call(
        _ffn_kernel,
        grid=(t // tm, d_ff // tf),
        in_specs=[pl.BlockSpec((tm, D_MODEL), lambda i, f: (i, 0)),
                  pl.BlockSpec((tm, D_MODEL), lambda i, f: (i, 0)),
                  pl.BlockSpec((D_MODEL, tf), lambda i, f: (0, f)),
                  pl.BlockSpec((D_MODEL, tf), lambda i, f: (0, f)),
                  pl.BlockSpec((tf, D_MODEL), lambda i, f: (f, 0))],
        out_specs=pl.BlockSpec((tm, D_MODEL), lambda i, f: (i, 0)),
        out_shape=jax.ShapeDtypeStruct((t, D_MODEL), F32),
        compiler_params=_cparams(("arbitrary", "arbitrary")),
        name="ffn",
    )(h, hn, w1, w3, w2)


def _dispatch_kernel(starts_ref, padded_ref, used_ref, pos_ref, src_ref, zero_ref, o_ref, sem, zsem):
    i = pl.program_id(0)
    tokens = pos_ref.shape[1] // TOP_K
    tm = zero_ref.shape[0]
    n_tiles = o_ref.shape[0] // tm

    def zero_tile(row0):
        return pltpu.make_async_copy(zero_ref, o_ref.at[pl.ds(pl.multiple_of(row0, tm), tm)], zsem)

    def tail_fill(e):
        return zero_tile(starts_ref[e] + padded_ref[e] - tm)

    @pl.when(i == 0)
    def _():
        for e in range(N_EXPERTS):
            pl.when(padded_ref[e] > 0)(lambda e=e: tail_fill(e).start())
        lax.fori_loop(used_ref[0], n_tiles, lambda j, c: (zero_tile(j * tm).start(), c)[1], 0)
        for e in range(N_EXPERTS):
            pl.when(padded_ref[e] > 0)(lambda e=e: tail_fill(e).wait())
        lax.fori_loop(used_ref[0], n_tiles, lambda j, c: (zero_tile(j * tm).wait(), c)[1], 0)

    def copy(r, k):
        return pltpu.make_async_copy(src_ref.at[pl.ds(r, 1)],
                                     o_ref.at[pl.ds(pos_ref[0, TOP_K * r + k], 1)], sem)

    def issue(r, c):
        for k in range(TOP_K):
            copy(r, k).start()
        return c

    def drain(r, c):
        for k in range(TOP_K):
            copy(r, k).wait()
        return c

    lax.fori_loop(0, tokens, issue, 0, unroll=GATHER_UNROLL)
    lax.fori_loop(0, tokens, drain, 0, unroll=GATHER_UNROLL)


def _dispatch(src, pos, starts, padded, n_used, n_rows, tm):
    t, width = src.shape
    tg = min(GATHER_TILE, t)
    grid_spec = pltpu.PrefetchScalarGridSpec(
        num_scalar_prefetch=3,
        grid=(t // tg,),
        in_specs=[pl.BlockSpec((None, 1, TOP_K * tg), lambda i, s, p, u: (i, 0, 0), memory_space=pltpu.SMEM),
                  pl.BlockSpec((tg, width), lambda i, s, p, u: (i, 0)),
                  pl.BlockSpec((tm, width), lambda i, s, p, u: (0, 0), pipeline_mode=pl.Buffered(1))],
        out_specs=pl.BlockSpec(memory_space=pl.ANY),
        scratch_shapes=[pltpu.SemaphoreType.DMA(()), pltpu.SemaphoreType.DMA(())],
    )
    return pl.pallas_call(
        _dispatch_kernel,
        grid_spec=grid_spec,
        out_shape=jax.ShapeDtypeStruct((n_rows, width), src.dtype),
        compiler_params=_cparams(("arbitrary",)),
        name="moe_dispatch",
    )(starts, padded, n_used, pos.reshape(t // tg, 1, TOP_K * tg), src, jnp.zeros((tm, width), src.dtype))


def _expert_kernel(te_ref, tv_ref, tx_ref, x_ref, w1_ref, w3_ref, w2_ref, o_ref, xb_ref):
    i = pl.program_id(0)
    f = pl.program_id(1)

    @pl.when(f == 0)
    def _():
        xb_ref[...] = x_ref[...].astype(BF16)
        o_ref[...] = jnp.zeros_like(o_ref)

    @pl.when(tv_ref[i] > 0)
    def _():
        xb = xb_ref[...]
        a1 = jnp.dot(xb, w1_ref[...], preferred_element_type=F32)
        a3 = jnp.dot(xb, w3_ref[...], preferred_element_type=F32)
        act = (a1 * jax.nn.sigmoid(a1) * a3).astype(BF16)
        o_ref[...] += jnp.dot(act, w2_ref[...], preferred_element_type=F32)


def _experts(xs, tile_expert, tile_valid, tile_src, w1, w3, w2, tm):
    n = xs.shape[0]
    d_ff = w1.shape[2]
    tf = EXPERT_COL_TILE
    nf = d_ff // tf
    fidx = lambda i, f, tv: jnp.where(tv[i] > 0, f, nf - 1)
    grid_spec = pltpu.PrefetchScalarGridSpec(
        num_scalar_prefetch=3,
        grid=(n // tm, nf),
        in_specs=[pl.BlockSpec((tm, D_MODEL), lambda i, f, te, tv, tx: (tx[i], 0)),
                  pl.BlockSpec((None, D_MODEL, tf), lambda i, f, te, tv, tx: (te[i], 0, fidx(i, f, tv))),
                  pl.BlockSpec((None, D_MODEL, tf), lambda i, f, te, tv, tx: (te[i], 0, fidx(i, f, tv))),
                  pl.BlockSpec((None, tf, D_MODEL), lambda i, f, te, tv, tx: (te[i], fidx(i, f, tv), 0))],
        out_specs=pl.BlockSpec((tm, D_MODEL), lambda i, f, te, tv, tx: (i, 0)),
        scratch_shapes=[pltpu.VMEM((tm, D_MODEL), BF16)],
    )
    return pl.pallas_call(
        _expert_kernel,
        grid_spec=grid_spec,
        out_shape=jax.ShapeDtypeStruct((n, D_MODEL), F32),
        compiler_params=_cparams(("arbitrary", "arbitrary")),
        name="moe_experts",
    )(tile_expert, tile_valid, tile_src, xs, w1, w3, w2)


def _combine_kernel(p0_ref, p1_ref, h_ref, gate_ref, ys_ref, o_ref, b0, b1, sem0, sem1):
    rows = o_ref.shape[0]

    def copies(r):
        return (pltpu.make_async_copy(ys_ref.at[pl.ds(p0_ref[0, r], 1)], b0.at[pl.ds(r, 1)], sem0),
                pltpu.make_async_copy(ys_ref.at[pl.ds(p1_ref[0, r], 1)], b1.at[pl.ds(r, 1)], sem1))

    def issue(r, c):
        c0, c1 = copies(r)
        c0.start()
        c1.start()
        return c

    def drain(r, c):
        c0, c1 = copies(r)
        c0.wait()
        c1.wait()
        return c

    lax.fori_loop(0, rows, issue, 0, unroll=GATHER_UNROLL)
    lax.fori_loop(0, rows, drain, 0, unroll=GATHER_UNROLL)
    gate = gate_ref[...]
    o_ref[...] = h_ref[...] + gate[:, 0:1] * b0[...] + gate[:, 1:2] * b1[...]


def _combine(h, gates, ys, pos0, pos1):
    t = h.shape[0]
    tg = min(GATHER_TILE, t)
    smem = pl.BlockSpec((None, 1, tg), lambda i: (i, 0, 0), memory_space=pltpu.SMEM)
    return pl.pallas_call(
        _combine_kernel,
        grid=(t // tg,),
        in_specs=[smem, smem,
                  pl.BlockSpec((tg, D_MODEL), lambda i: (i, 0)),
                  pl.BlockSpec((tg, ROUTER_LANES), lambda i: (i, 0)),
                  pl.BlockSpec(memory_space=pl.ANY)],
        out_specs=pl.BlockSpec((tg, D_MODEL), lambda i: (i, 0)),
        out_shape=jax.ShapeDtypeStruct((t, D_MODEL), F32),
        scratch_shapes=[pltpu.VMEM((tg, D_MODEL), F32), pltpu.VMEM((tg, D_MODEL), F32),
                        pltpu.SemaphoreType.DMA(()), pltpu.SemaphoreType.DMA(())],
        compiler_params=_cparams(("arbitrary",)),
        name="moe_combine",
    )(pos0.reshape(t // tg, 1, tg), pos1.reshape(t // tg, 1, tg), h, gates, ys)


def _moe(h, hn, idx, gates, w1, w3, w2):
    t = h.shape[0]
    tm = min(ROW_TILE, t)
    n_tiles = (t * TOP_K) // tm + N_EXPERTS
    n_rows = n_tiles * tm
    flat_e = idx[:, :TOP_K].reshape(-1)
    onehot = (flat_e[:, None] == jnp.arange(N_EXPERTS)[None, :]).astype(jnp.int32)
    rank = jnp.cumsum(onehot, axis=0) - onehot
    counts = jnp.sum(onehot, axis=0)
    padded = ((counts + tm - 1) // tm) * tm
    ends = jnp.cumsum(padded)
    starts = ends - padded
    pos = jnp.sum(onehot * (starts[None, :] + rank), axis=1).astype(jnp.int32)
    tile_idx = jnp.arange(n_tiles, dtype=jnp.int32)
    tile_start = tile_idx * tm
    tile_valid = (tile_start < ends[-1]).astype(jnp.int32)
    tile_expert = jnp.minimum(jnp.sum((tile_start[:, None] >= ends[None, :]).astype(jnp.int32), axis=1),
                              N_EXPERTS - 1)
    last_tile = jnp.maximum(ends[-1] // tm - 1, 0).astype(jnp.int32)
    tile_expert = jnp.where(tile_valid > 0, tile_expert, tile_expert[last_tile]).astype(jnp.int32)
    tile_src = jnp.where(tile_valid > 0, tile_idx, last_tile)
    n_used = (ends[-1:] // tm).astype(jnp.int32)
    xs = _dispatch(hn, pos, starts.astype(jnp.int32), padded.astype(jnp.int32), n_used, n_rows, tm)
    ys = _experts(xs, tile_expert, tile_valid, tile_src, w1, w3, w2, tm)
    pos = pos.reshape(t, TOP_K)
    return _combine(h, gates, ys, pos[:, 0], pos[:, 1])


def _permute_in_cols(w):
    o = V_END
    wd = RWKV_WIDTH
    r = (o, o + wd)
    xw = (r[1], r[1] + RWKV_DECAY_LORA)
    k = (xw[1], xw[1] + wd)
    v = (k[1], k[1] + wd)
    xa = (v[1], v[1] + RWKV_A_LORA)
    xg = (xa[1], xa[1] + RWKV_GATE_LORA)
    order = [(0, o), r, k, v, xg, xw, xa, (RWKV_END, IN_COLS)]
    return jnp.concatenate([w[..., a:b] for a, b in order], axis=-1)


def kernel(x, norm1_g, w_in, q_norm_g, k_norm_g, attn_sinks, rel_bias, rwkv_mu, rwkv_w0, rwkv_w2, rwkv_a0, rwkv_a2, rwkv_g2, rwkv_k_k, rwkv_k_a, rwkv_r_k, rwkv_ln_w, rwkv_ln_b, ssm_lambda_re, ssm_lambda_im, ssm_b_re, ssm_b_im, ssm_c_re, ssm_c_im, ssm_d, ssm_log_dt, ssm_glu_w, ssm_glu_b, attn_out_g, ssm_out_g, w_out, norm2_g, ffn_w1, ffn_w3, ffn_w2, moe_router, moe_w1, moe_w3, moe_w2):
    batch, seq, _ = x.shape
    depth = w_in.shape[0]
    h = x.reshape(batch * seq, D_MODEL)
    for i in range(depth):
        w_in_i = _permute_in_cols(w_in[i]).astype(BF16)
        mu_i = _permute_in_cols(jnp.pad(rwkv_mu[i], (V_END, SSM_WIDTH)))[V_END:RWKV_END]
        q, kv, zr, zs = _in_proj(h, norm1_g[i], w_in_i)
        attn = _attention(q, kv, batch, seq, q_norm_g[i], k_norm_g[i], attn_sinks[i], rel_bias,
                          attn_out_g[i])
        rw = _rwkv(zr, batch, seq, mu_i, rwkv_w0[i], rwkv_w2[i], rwkv_a0[i], rwkv_a2[i], rwkv_g2[i],
                   rwkv_k_k[i], rwkv_k_a[i], rwkv_r_k[i], rwkv_ln_w[i], rwkv_ln_b[i])
        ss = _s5(zs, batch, seq, ssm_lambda_re[i], ssm_lambda_im[i], ssm_b_re[i], ssm_b_im[i],
                 ssm_c_re[i], ssm_c_im[i], ssm_d[i], ssm_log_dt[i], ssm_glu_w[i], ssm_glu_b[i],
                 ssm_out_g[i])
        j = i // 2
        if i % 2 == 0:
            h, hn = _out_proj(h, attn, rw, ss, w_out[i].astype(BF16), norm2_g[i])
            h = _ffn(h, hn, ffn_w1[j].astype(BF16), ffn_w3[j].astype(BF16), ffn_w2[j].astype(BF16))
        else:
            h, hn, idx, gates = _out_proj(h, attn, rw, ss, w_out[i].astype(BF16), norm2_g[i],
                                          router=moe_router[j])
            h = _moe(h, hn, idx, gates, moe_w1[j].astype(BF16), moe_w3[j].astype(BF16),
                     moe_w2[j].astype(BF16))
    return h.reshape(batch, seq, D_MODEL)
```
